```python
import math
import jax, jax.numpy as jnp
from jax import lax
import numpy as np

D_MODEL = 1024
BATCH = 16
SEQ = 2048
DEPTH = 4

CTX_LEN = 256
GRID_W = 64
D_MIX = D_MODEL
D_S5 = D_MIX // 2
D_LRU = D_MIX - D_S5
S5_GROUP_CH = 16
S5_GROUPS = D_S5 // S5_GROUP_CH
S5_STATE = 64
LRU_HEADS = 8
LRU_HEAD_DIM = D_LRU // LRU_HEADS
LRU_CONV = 4
LRU_PAD = (2, 1)
LRU_C = 8.0
D_FF = 2816
FFN_CONV = 3
N_DIR = 2
N_MOD = 6
LN_EPS = 1e-6
DEEPNORM_ALPHA = (2 * DEPTH) ** 0.25
DEEPNORM_BETA = (8 * DEPTH) ** -0.25

kernel_name = "hybrid_s5_rglru_convffn_deepnorm_dit"


def layer_norm(x, g=None, b=None):
    xf = x.astype(jnp.float32)
    mu = jnp.mean(xf, axis=-1, keepdims=True)
    var = jnp.mean(jnp.square(xf - mu), axis=-1, keepdims=True)
    y = ((xf - mu) * lax.rsqrt(var + LN_EPS)).astype(x.dtype)
    if g is not None:
        y = y * g + b
    return y


def ada_params(cond, w_mod, b_mod):
    m = jax.nn.silu(cond) @ w_mod + b_mod
    return jnp.split(m[..., None, :], N_MOD, axis=-1)


def modulate(x, shift, scale):
    return layer_norm(x) * (1.0 + scale) + shift


def linear_scan(a, b, h0, reverse):
    if h0 is not None:
        idx = -1 if reverse else 0
        b = b.at[:, idx].add(a[:, idx] * h0)

    def combine(left, right):
        return left[0] * right[0], right[0] * left[1] + right[1]

    _, h = lax.associative_scan(combine, (a, b), reverse=reverse, axis=1)
    return h


def dwconv_seq(u, w, b, pad):
    c = u.shape[-1]
    out = lax.conv_general_dilated(u, w[:, None, :], (1,), (pad,),
                                   dimension_numbers=("NWC", "WIO", "NWC"),
                                   feature_group_count=c)
    return out + b


def dwconv_grid(u, w, b, rows):
    bsz, length, c = u.shape
    img = u.reshape(bsz, rows, GRID_W, c)
    out = lax.conv_general_dilated(img, w[:, :, None, :], (1, 1), ((1, 1), (1, 1)),
                                   dimension_numbers=("NHWC", "HWIO", "NHWC"),
                                   feature_group_count=c)
    return out.reshape(bsz, length, c) + b


def s5_direction(ug, s0, lam_re, lam_im, log_dt, b_re, b_im, reverse):
    lam = lax.complex(lam_re.astype(jnp.float32), lam_im.astype(jnp.float32))
    dt = jnp.exp(log_dt.astype(jnp.float32))[:, None]
    lam_bar = jnp.exp(lam * dt)
    b_mat = lax.complex(b_re.astype(jnp.float32), b_im.astype(jnp.float32))
    b_bar = ((lam_bar - 1.0) / lam)[..., None] * b_mat
    bu = jnp.einsum("blgc,gpc->blgp", ug.astype(jnp.complex64), b_bar)
    return linear_scan(jnp.broadcast_to(lam_bar, bu.shape), bu, s0, reverse)


def s5_readout(states, c_re, c_im):
    c_mat = lax.complex(c_re.astype(jnp.float32), c_im.astype(jnp.float32))
    return jnp.real(jnp.einsum("blgp,gcp->blgc", states, c_mat))


def rglru_direction(xc, h0, w_a, b_a, w_x, b_x, lam, reverse):
    bsz, length, dl = xc.shape
    xh = xc.reshape(bsz, length, LRU_HEADS, LRU_HEAD_DIM)
    r = jax.nn.sigmoid(jnp.einsum("blhi,hij->blhj", xh, w_a).reshape(bsz, length, dl) + b_a)
    i = jax.nn.sigmoid(jnp.einsum("blhi,hij->blhj", xh, w_x).reshape(bsz, length, dl) + b_x)
    log_a = (-LRU_C * r.astype(jnp.float32)) * jax.nn.softplus(-lam.astype(jnp.float32))
    a = jnp.exp(log_a)
    b = jnp.sqrt(-jnp.expm1(2.0 * log_a)) * (i * xc).astype(jnp.float32)
    return linear_scan(a, b, h0, reverse)


def token_mixers(h, inits, lw, want_out, want_finals):
    bsz, length, _ = h.shape
    proj = h @ lw["w_in"]
    u_s5 = proj[..., :D_S5]
    x_lru = proj[..., D_S5:D_S5 + D_LRU]
    g_lru = proj[..., D_S5 + D_LRU:]
    if inits is None:
        inits = (None, None, None, None)
    ug = u_s5.astype(jnp.float32).reshape(bsz, length, S5_GROUPS, S5_GROUP_CH)
    s5_states = [s5_direction(ug, inits[d], lw["s5_lam_re"][d], lw["s5_lam_im"][d], lw["s5_log_dt"][d],
                              lw["s5_b_re"][d], lw["s5_b_im"][d], d == 1) for d in range(N_DIR)]
    xc = dwconv_seq(x_lru, lw["lru_conv_w"], lw["lru_conv_b"], LRU_PAD)
    lru_states = [rglru_direction(xc, inits[2 + d], lw["lru_w_a"][d], lw["lru_b_a"][d], lw["lru_w_x"][d],
                                  lw["lru_b_x"][d], lw["lru_lam"][d], d == 1) for d in range(N_DIR)]
    finals = None
    if want_finals:
        finals = (s5_states[0][:, -1], s5_states[1][:, 0], lru_states[0][:, -1], lru_states[1][:, 0])
    if not want_out:
        return None, finals
    y = (s5_readout(s5_states[0], lw["s5_c_re"][0], lw["s5_c_im"][0])
         + s5_readout(s5_states[1], lw["s5_c_re"][1], lw["s5_c_im"][1])
         + lw["s5_d"].astype(jnp.float32).reshape(S5_GROUPS, S5_GROUP_CH) * ug)
    y = jax.nn.gelu(y.reshape(bsz, length, D_S5))
    y = y * jax.nn.sigmoid(y @ lw["s5_w_glu"] + lw["s5_b_glu"])
    y_s5 = y.astype(h.dtype)
    y_lru = (lru_states[0] + lru_states[1]).astype(h.dtype) * jax.nn.gelu(g_lru)
    out = jnp.concatenate([y_s5, y_lru], axis=-1) @ lw["w_out"]
    return out, finals


def conv_ffn(h, w_up, conv_w, conv_b, w_down, rows):
    u, v = jnp.split(h @ w_up, 2, axis=-1)
    if rows is None:
        u = dwconv_seq(u, conv_w[1], conv_b, (1, 1))
    else:
        u = dwconv_grid(u, conv_w, conv_b, rows)
    return (jax.nn.gelu(u) * v) @ w_down


def setup_inputs(seed: int = 0) -> dict:
    key = jax.random.key(seed)
    ks = iter(jax.random.split(key, 48))
    f32 = jnp.float32

    def nrm(shape, scale):
        return scale * jax.random.normal(next(ks), shape, f32)

    x = nrm((BATCH, SEQ, D_MODEL), 1.0)
    c = nrm((BATCH, D_MODEL), 1.0)
    ctx = nrm((BATCH, CTX_LEN, D_MODEL), 1.0)
    c_ctx = nrm((D_MODEL,), 1.0)
    w_mod = nrm((DEPTH, D_MODEL, N_MOD * D_MODEL), 0.5 * D_MODEL ** -0.5)
    b_mod = nrm((DEPTH, N_MOD * D_MODEL), 0.01)
    w_in = nrm((DEPTH, D_MODEL, D_S5 + 2 * D_LRU), D_MODEL ** -0.5)
    s5_shape = (DEPTH, N_DIR, S5_GROUPS, S5_STATE)
    s5_lam_re = -0.5 + nrm(s5_shape, 0.01)
    s5_lam_im = math.pi * jnp.arange(S5_STATE, dtype=f32) + nrm(s5_shape, 0.01)
    s5_log_dt = jax.random.uniform(next(ks), (DEPTH, N_DIR, S5_GROUPS), f32, math.log(1e-3), math.log(1e-1))
    s5_b_re = nrm((DEPTH, N_DIR, S5_GROUPS, S5_STATE, S5_GROUP_CH), (2 * S5_GROUP_CH) ** -0.5)
    s5_b_im = nrm((DEPTH, N_DIR, S5_GROUPS, S5_STATE, S5_GROUP_CH), (2 * S5_GROUP_CH) ** -0.5)
    s5_c_re = nrm((DEPTH, N_DIR, S5_GROUPS, S5_GROUP_CH, S5_STATE), (2 * S5_STATE) ** -0.5)
    s5_c_im = nrm((DEPTH, N_DIR, S5_GROUPS, S5_GROUP_CH, S5_STATE), (2 * S5_STATE) ** -0.5)
    s5_d = nrm((DEPTH, D_S5), 1.0)
    s5_w_glu = nrm((DEPTH, D_S5, D_S5), D_S5 ** -0.5)
    s5_b_glu = nrm((DEPTH, D_S5), 0.01)
    lru_conv_w = nrm((DEPTH, LRU_CONV, D_LRU), LRU_CONV ** -0.5)
    lru_conv_b = nrm((DEPTH, D_LRU), 0.01)
    blk = (DEPTH, N_DIR, LRU_HEADS, LRU_HEAD_DIM, LRU_HEAD_DIM)
    lru_w_a = nrm(blk, LRU_HEAD_DIM ** -0.5)
    lru_b_a = nrm((DEPTH, N_DIR, D_LRU), 0.01)
    lru_w_x = nrm(blk, LRU_HEAD_DIM ** -0.5)
    lru_b_x = nrm((DEPTH, N_DIR, D_LRU), 0.01)
    a_pow = jax.random.uniform(next(ks), (DEPTH, N_DIR, D_LRU), f32, 0.9, 0.999)
    s = a_pow ** (1.0 / LRU_C)
    lru_lam = jnp.log(s) - jnp.log1p(-s)
    w_out = nrm((DEPTH, D_MIX, D_MODEL), DEEPNORM_BETA * D_MIX ** -0.5)
    ln1_g = 1.0 + nrm((DEPTH, D_MODEL), 0.01)
    ln1_b = nrm((DEPTH, D_MODEL), 0.01)
    ffn_w_up = nrm((DEPTH, D_MODEL, 2 * D_FF), D_MODEL ** -0.5)
    ffn_conv_w = nrm((DEPTH, FFN_CONV, FFN_CONV, D_FF), 1.0 / FFN_CONV)
    ffn_conv_b = nrm((DEPTH, D_FF), 0.01)
    ffn_w_down = nrm((DEPTH, D_FF, D_MODEL), DEEPNORM_BETA * D_FF ** -0.5)
    ln2_g = 1.0 + nrm((DEPTH, D_MODEL), 0.01)
    ln2_b = nrm((DEPTH, D_MODEL), 0.01)
    return {"x": x, "c": c, "ctx": ctx, "c_ctx": c_ctx, "w_mod": w_mod, "b_mod": b_mod, "w_in": w_in,
            "s5_lam_re": s5_lam_re, "s5_lam_im": s5_lam_im, "s5_log_dt": s5_log_dt,
            "s5_b_re": s5_b_re, "s5_b_im": s5_b_im, "s5_c_re": s5_c_re, "s5_c_im": s5_c_im,
            "s5_d": s5_d, "s5_w_glu": s5_w_glu, "s5_b_glu": s5_b_glu,
            "lru_conv_w": lru_conv_w, "lru_conv_b": lru_conv_b, "lru_w_a": lru_w_a, "lru_b_a": lru_b_a,
            "lru_w_x": lru_w_x, "lru_b_x": lru_b_x, "lru_lam": lru_lam, "w_out": w_out,
            "ln1_g": ln1_g, "ln1_b": ln1_b, "ffn_w_up": ffn_w_up, "ffn_conv_w": ffn_conv_w,
            "ffn_conv_b": ffn_conv_b, "ffn_w_down": ffn_w_down, "ln2_g": ln2_g, "ln2_b": ln2_b}


def reference(x, c, ctx, c_ctx, w_mod, b_mod, w_in, s5_lam_re, s5_lam_im, s5_log_dt, s5_b_re, s5_b_im,
              s5_c_re, s5_c_im, s5_d, s5_w_glu, s5_b_glu, lru_conv_w, lru_conv_b, lru_w_a, lru_b_a,
              lru_w_x, lru_b_x, lru_lam, w_out, ln1_g, ln1_b, ffn_w_up, ffn_conv_w, ffn_conv_b,
              ffn_w_down, ln2_g, ln2_b):
    rows = x.shape[1] // GRID_W
    xs = x
    cs = ctx
    for l in range(DEPTH):
        last = l == DEPTH - 1
        lw = {"w_in": w_in[l], "s5_lam_re": s5_lam_re[l], "s5_lam_im": s5_lam_im[l], "s5_log_dt": s5_log_dt[l],
              "s5_b_re": s5_b_re[l], "s5_b_im": s5_b_im[l], "s5_c_re": s5_c_re[l], "s5_c_im": s5_c_im[l],
              "s5_d": s5_d[l], "s5_w_glu": s5_w_glu[l], "s5_b_glu": s5_b_glu[l],
              "lru_conv_w": lru_conv_w[l], "lru_conv_b": lru_conv_b[l], "lru_w_a": lru_w_a[l],
              "lru_b_a": lru_b_a[l], "lru_w_x": lru_w_x[l], "lru_b_x": lru_b_x[l], "lru_lam": lru_lam[l],
              "w_out": w_out[l]}
        sh1, sc1, g1, sh2, sc2, g2 = ada_params(c, w_mod[l], b_mod[l])
        csh1, csc1, cg1, csh2, csc2, cg2 = ada_params(c_ctx, w_mod[l], b_mod[l])

        mix_c, finals = token_mixers(modulate(cs, csh1, csc1), None, lw, want_out=not last, want_finals=True)
        mix_x, _ = token_mixers(modulate(xs, sh1, sc1), finals, lw, want_out=True, want_finals=False)

        xs = layer_norm(DEEPNORM_ALPHA * xs + g1 * mix_x, ln1_g[l], ln1_b[l])
        ffn_x = conv_ffn(modulate(xs, sh2, sc2), ffn_w_up[l], ffn_conv_w[l], ffn_conv_b[l], ffn_w_down[l], rows)
        xs = layer_norm(DEEPNORM_ALPHA * xs + g2 * ffn_x, ln2_g[l], ln2_b[l])

        if not last:
            cs = layer_norm(DEEPNORM_ALPHA * cs + cg1 * mix_c, ln1_g[l], ln1_b[l])
            ffn_c = conv_ffn(modulate(cs, csh2, csc2), ffn_w_up[l], ffn_conv_w[l], ffn_conv_b[l], ffn_w_down[l], None)
            cs = layer_norm(DEEPNORM_ALPHA * cs + cg2 * ffn_c, ln2_g[l], ln2_b[l])
    return xs
```

```python
import functools
import math

import jax
import jax.numpy as jnp
from jax import lax
from jax.experimental import pallas as pl
from jax.experimental.pallas import tpu as pltpu

F32 = jnp.float32
BF16 = jnp.bfloat16

D_MODEL = 1024
BATCH = 16
DEPTH = 4
GRID_W = 64
D_S5 = 512
D_LRU = 512
S5_GROUP_CH = 16
S5_GROUPS = 32
S5_STATE = 64
LRU_HEADS = 8
LRU_HEAD_DIM = 64
LRU_C = 8.0
D_FF = 2816
N_MOD = 6
LN_EPS = 1e-6
DEEPNORM_ALPHA = (2 * DEPTH) ** 0.25

LANES = 128
GROUPS_PER_LANE_BLOCK = LANES // S5_GROUP_CH
S5_LANE_BLOCKS = D_S5 // LANES
S5_STATE_COLS = 2 * GROUPS_PER_LANE_BLOCK * S5_STATE
S5_CHUNK = 8
S5_TK = S5_CHUNK * LANES
MOD_ROWS = 24
FF_CHUNK = 256
VMEM_LIMIT = 56 * 1024 * 1024


def _cparams(sem):
    return pltpu.CompilerParams(dimension_semantics=sem, vmem_limit_bytes=VMEM_LIMIT)


def _sigmoid(x):
    return 0.5 * jnp.tanh(0.5 * x) + 0.5


def _gelu(x):
    return 0.5 * x * (1.0 + jnp.tanh(0.7978845608028654 * (x + 0.044715 * (x * x * x))))


def _ln(x):
    mu = jnp.mean(x, axis=-1, keepdims=True)
    xc = x - mu
    var = jnp.mean(xc * xc, axis=-1, keepdims=True)
    return xc * lax.rsqrt(var + LN_EPS)


def _per_batch(x, fn):
    rows, d = x.shape
    return fn(x.reshape(rows // BATCH, BATCH, d)).reshape(rows, d)


def _modulate(x, shift, scale):
    return _per_batch(_ln(x), lambda h: h * (1.0 + scale)[None] + shift[None])


def _mod_kernel(c_ref, w_ref, b_ref, o_ref):
    c = c_ref[...]
    s = c * _sigmoid(c)
    o_ref[0] = jnp.dot(s, w_ref[0], preferred_element_type=F32) + b_ref[0]


def _mod_call(c_all, w_mod, b_mod):
    tn = 1536
    return pl.pallas_call(
        _mod_kernel,
        grid=(DEPTH, N_MOD * D_MODEL // tn),
        in_specs=[pl.BlockSpec((MOD_ROWS, D_MODEL), lambda l, j: (0, 0)),
                  pl.BlockSpec((1, D_MODEL, tn), lambda l, j: (l, 0, j)),
                  pl.BlockSpec((1, 1, tn), lambda l, j: (l, 0, j))],
        out_specs=pl.BlockSpec((1, MOD_ROWS, tn), lambda l, j: (l, 0, j)),
        out_shape=jax.ShapeDtypeStruct((DEPTH, MOD_ROWS, N_MOD * D_MODEL), F32),
        compiler_params=_cparams(("parallel", "parallel")),
        name="mod",
    )(c_all, w_mod, b_mod.reshape(DEPTH, 1, N_MOD * D_MODEL))


def _mix_in_kernel(x_ref, sh_ref, sc_ref, w_ref, u_ref, xl_ref, gl_ref):
    h = _modulate(x_ref[...], sh_ref[...], sc_ref[...]).astype(BF16)
    p = jnp.dot(h, w_ref[...], preferred_element_type=F32)
    u_ref[...] = p[:, :D_S5].astype(BF16)
    xl_ref[...] = p[:, D_S5:D_S5 + D_LRU].astype(BF16)
    gl_ref[...] = p[:, D_S5 + D_LRU:].astype(BF16)


def _mix_in_call(xs, shift, scale, w_in):
    n = xs.shape[0]
    tm = 1024
    vec = pl.BlockSpec((BATCH, D_MODEL), lambda i: (0, 0))
    out = pl.BlockSpec((tm, D_S5), lambda i: (i, 0))
    return pl.pallas_call(
        _mix_in_kernel,
        grid=(n // tm,),
        in_specs=[pl.BlockSpec((tm, D_MODEL), lambda i: (i, 0)), vec, vec,
                  pl.BlockSpec((D_MODEL, D_S5 + 2 * D_LRU), lambda i: (0, 0))],
        out_specs=[out, out, out],
        out_shape=[jax.ShapeDtypeStruct((n, D_S5), BF16)] * 3,
        compiler_params=_cparams(("parallel",)),
        name="mix_in",
    )(xs, shift, scale, w_in)


def _s5_matrices(lam_re, lam_im, log_dt, b_re, b_im, c_re, c_im):
    T = S5_CHUNK
    lam = lax.complex(lam_re, lam_im)
    lam_dt = lam * jnp.exp(log_dt)[..., None]
    lam_bar = jnp.exp(lam_dt)
    b_bar = ((lam_bar - 1.0) / lam)[..., None] * lax.complex(b_re, b_im)
    c_mat = lax.complex(c_re, c_im)
    steps = jnp.arange(T + 1, dtype=F32)
    pw = jnp.exp(steps[:, None, None, None] * lam_dt[None])
    kd = jnp.real(jnp.einsum("dgop,ndgp,dgpi->ndgoi", c_mat, pw[:T], b_bar))
    eye = jnp.eye(GROUPS_PER_LANE_BLOCK, dtype=F32)
    nb, gpb = S5_LANE_BLOCKS, GROUPS_PER_LANE_BLOCK

    ar = jnp.arange(T)
    lag = jnp.stack([ar[None, :] - ar[:, None], ar[:, None] - ar[None, :]])
    km = jnp.stack([kd[:, d][jnp.clip(lag[d], 0, T - 1)] for d in range(2)])
    km = km * (lag >= 0)[..., None, None, None]
    km = km.reshape(2, T, T, nb, gpb, S5_GROUP_CH, S5_GROUP_CH)
    m = jnp.einsum("dabjgoi,gh->djagibho", km, eye).reshape(2, nb, S5_TK, S5_TK)

    e_in = jnp.stack([T - 1 - ar, ar])
    wc = jnp.stack([pw[e_in[d], d] for d in range(2)])[..., None] * b_bar[:, None]
    wri = jnp.stack([jnp.real(wc), jnp.imag(wc)]).reshape(2, 2, T, nb, gpb, S5_STATE, S5_GROUP_CH)
    w = jnp.einsum("rdtjgpi,gh->djtgirhp", wri, eye).reshape(2, nb, S5_TK, S5_STATE_COLS)

    f_out = jnp.stack([ar + 1, T - ar])
    zc = c_mat[:, None] * jnp.stack([pw[f_out[d], d] for d in range(2)])[:, :, :, None, :]
    zri = jnp.stack([jnp.real(zc), -jnp.imag(zc)]).reshape(2, 2, T, nb, gpb, S5_GROUP_CH, S5_STATE)
    z = jnp.einsum("rdtjgop,gh->djrgptho", zri, eye).reshape(2, nb, S5_STATE_COLS, S5_TK)

    a_t = pw[T].reshape(2, nb, gpb * S5_STATE)
    a = jnp.stack([jnp.real(a_t), jnp.imag(a_t)], axis=2)
    return m.astype(BF16), w.astype(BF16), z.astype(BF16), a


def _s5_kernel(u_ref, m_ref, w_ref, z_ref, a_ref, init_ref, y_ref, fin_ref, s_ref, v_ref, sp_ref, *, kc, n_tiles):
    d = pl.program_id(0)
    rt = pl.program_id(2)
    half = S5_STATE_COLS // 2

    @pl.when(rt == 0)
    def _():
        s_ref[...] = init_ref[0, 0]

    ucat = jnp.concatenate(
        [u_ref[:, BATCH * t:BATCH * (t + 1), :].reshape(kc * BATCH, LANES) for t in range(S5_CHUNK)], axis=1)
    v_ref[...] = jnp.dot(ucat, w_ref[0, 0], preferred_element_type=F32)

    a_re = jnp.broadcast_to(a_ref[0, 0, 0:1, :], (BATCH, half))
    a_im = jnp.broadcast_to(a_ref[0, 0, 1:2, :], (BATCH, half))

    def step(i, carry):
        s_re, s_im = carry
        k = i + d * (kc - 1 - 2 * i)
        off = pl.multiple_of(k * BATCH, BATCH)
        sp_ref[pl.ds(off, BATCH), :half] = s_re
        sp_ref[pl.ds(off, BATCH), half:] = s_im
        v = v_ref[pl.ds(off, BATCH), :]
        n_re = a_re * s_re - a_im * s_im + v[:, :half]
        n_im = a_re * s_im + a_im * s_re + v[:, half:]
        return n_re, n_im

    s_re, s_im = lax.fori_loop(0, kc, step, (s_ref[:, :half], s_ref[:, half:]))
    s_ref[:, :half] = s_re
    s_ref[:, half:] = s_im

    y = jnp.dot(ucat, m_ref[0, 0], preferred_element_type=F32)
    y = y + jnp.dot(sp_ref[...].astype(BF16), z_ref[0, 0], preferred_element_type=F32)
    for t in range(S5_CHUNK):
        y_ref[0, :, BATCH * t:BATCH * (t + 1), :] = (
            y[:, LANES * t:LANES * (t + 1)].reshape(kc, BATCH, LANES).astype(BF16))

    @pl.when(rt == n_tiles - 1)
    def _():
        fin_ref[0, 0] = s_ref[...]


def _s5_call(u, mats, init):
    m, w, z, a = mats
    n = u.shape[0]
    chunk_rows = S5_CHUNK * BATCH
    n_chunks = n // chunk_rows
    kc = min(64, n_chunks)
    n_tiles = n_chunks // kc
    nb = S5_LANE_BLOCKS

    def tile(d, rt):
        return rt + d * (n_tiles - 1 - 2 * rt)

    per_dir = lambda shape: pl.BlockSpec((1, 1) + shape, lambda d, j, rt: (d, j, 0, 0))
    y, fin = pl.pallas_call(
        functools.partial(_s5_kernel, kc=kc, n_tiles=n_tiles),
        grid=(2, nb, n_tiles),
        in_specs=[pl.BlockSpec((kc, chunk_rows, LANES), lambda d, j, rt: (tile(d, rt), 0, j)),
                  per_dir((S5_TK, S5_TK)), per_dir((S5_TK, S5_STATE_COLS)), per_dir((S5_STATE_COLS, S5_TK)),
                  per_dir((2, S5_STATE_COLS // 2)), per_dir((BATCH, S5_STATE_COLS))],
        out_specs=[pl.BlockSpec((1, kc, chunk_rows, LANES), lambda d, j, rt: (d, tile(d, rt), 0, j)),
                   per_dir((BATCH, S5_STATE_COLS))],
        out_shape=[jax.ShapeDtypeStruct((2, n_chunks, chunk_rows, D_S5), BF16),
                   jax.ShapeDtypeStruct((2, nb, BATCH, S5_STATE_COLS), F32)],
        scratch_shapes=[pltpu.VMEM((BATCH, S5_STATE_COLS), F32),
                        pltpu.VMEM((kc * BATCH, S5_STATE_COLS), F32),
                        pltpu.VMEM((kc * BATCH, S5_STATE_COLS), F32)],
        compiler_params=_cparams(("arbitrary", "arbitrary", "arbitrary")),
        name="s5",
    )(u.reshape(n_chunks, chunk_rows, D_S5), m, w, z, a, init)
    return y.reshape(2, n, D_S5), fin


LRU_HALO_PREV = 2 * BATCH
LRU_HALO_NEXT = BATCH


def _lru_kernel(xp_ref, xc_ref, xn_ref, cw_ref, cb_ref, wg_ref, bg_ref, lam_ref, init_ref,
                h_ref, fin_ref, xpad_ref, a_ref, b_ref, hst_ref, *, tm, n_tiles):
    d = pl.program_id(0)
    i = pl.program_id(1)
    te = i + d * (n_tiles - 1 - 2 * i)

    @pl.when(i == 0)
    def _():
        hst_ref[...] = init_ref[0]

    xpad_ref[0:LRU_HALO_PREV, :] = jnp.where(te > 0, xp_ref[...].astype(F32), 0.0)
    xpad_ref[LRU_HALO_PREV:LRU_HALO_PREV + tm, :] = xc_ref[...].astype(F32)
    xpad_ref[LRU_HALO_PREV + tm:, :] = jnp.where(te < n_tiles - 1, xn_ref[...].astype(F32), 0.0)

    xc = cb_ref[...]
    for k in range(4):
        xc = xc + cw_ref[k:k + 1, :] * xpad_ref[BATCH * k:BATCH * k + tm, :]

    gates = jnp.dot(xc.astype(BF16), wg_ref[0], preferred_element_type=F32) + bg_ref[0]
    r = _sigmoid(gates[:, :D_LRU])
    ig = _sigmoid(gates[:, D_LRU:])
    z = -lam_ref[0]
    softplus = jnp.maximum(z, 0.0) + jnp.log1p(jnp.exp(-jnp.abs(z)))
    log_a = (-LRU_C * r) * softplus
    a = jnp.exp(log_a)
    a_ref[...] = a
    b_ref[...] = jnp.sqrt(-jnp.tanh(log_a) * (a * a + 1.0)) * (ig * xc)

    steps = tm // BATCH

    def step(j, h):
        k = j + d * (steps - 1 - 2 * j)
        off = pl.multiple_of(k * BATCH, BATCH)
        h = a_ref[pl.ds(off, BATCH), :] * h + b_ref[pl.ds(off, BATCH), :]
        b_ref[pl.ds(off, BATCH), :] = h
        return h

    h = lax.fori_loop(0, steps, step, hst_ref[...])
    hst_ref[...] = h
    h_ref[0] = b_ref[...].astype(BF16)

    @pl.when(i == n_tiles - 1)
    def _():
        fin_ref[0] = h


def _lru_call(xl, conv_w, conv_b, wg, bg, lam, init):
    n = xl.shape[0]
    tm = 2048
    n_tiles = n // tm

    def tile(d, i):
        return i + d * (n_tiles - 1 - 2 * i)

    prev_blocks = tm // LRU_HALO_PREV
    next_blocks = tm // LRU_HALO_NEXT
    full = lambda shape: pl.BlockSpec(shape, lambda d, i: (0,) * len(shape))
    per_dir = lambda shape: pl.BlockSpec((1,) + shape, lambda d, i: (d,) + (0,) * len(shape))
    return pl.pallas_call(
        functools.partial(_lru_kernel, tm=tm, n_tiles=n_tiles),
        grid=(2, n_tiles),
        in_specs=[pl.BlockSpec((LRU_HALO_PREV, D_LRU),
                               lambda d, i: (jnp.maximum(tile(d, i) * prev_blocks - 1, 0), 0)),
                  pl.BlockSpec((tm, D_LRU), lambda d, i: (tile(d, i), 0)),
                  pl.BlockSpec((LRU_HALO_NEXT, D_LRU),
                               lambda d, i: (jnp.minimum((tile(d, i) + 1) * next_blocks, n // LRU_HALO_NEXT - 1), 0)),
                  full((4, D_LRU)), full((1, D_LRU)),
                  per_dir((D_LRU, 2 * D_LRU)), per_dir((1, 2 * D_LRU)), per_dir((1, D_LRU)),
                  per_dir((BATCH, D_LRU))],
        out_specs=[pl.BlockSpec((1, tm, D_LRU), lambda d, i: (d, tile(d, i), 0)),
                   per_dir((BATCH, D_LRU))],
        out_shape=[jax.ShapeDtypeStruct((2, n, D_LRU), BF16),
                   jax.ShapeDtypeStruct((2, BATCH, D_LRU), F32)],
        scratch_shapes=[pltpu.VMEM((LRU_HALO_PREV + tm + LRU_HALO_NEXT, D_LRU), F32),
                        pltpu.VMEM((tm, D_LRU), F32),
                        pltpu.VMEM((tm, D_LRU), F32),
                        pltpu.VMEM((BATCH, D_LRU), F32)],
        compiler_params=_cparams(("arbitrary", "arbitrary")),
        name="lru",
    )(xl, xl, xl, conv_w, conv_b, wg, bg, lam, init)


def _mix_out_kernel(ys_ref, u_ref, hl_ref, gl_ref, xs_ref, d_ref, wglu_ref, bglu_ref, wout_ref,
                    g1_ref, lng_ref, lnb_ref, sh2_ref, sc2_ref, xo_ref, h2_ref):
    y = ys_ref[0].astype(F32) + ys_ref[1].astype(F32) + d_ref[...] * u_ref[...].astype(F32)
    y = _gelu(y)
    y = y * _sigmoid(jnp.dot(y.astype(BF16), wglu_ref[...], preferred_element_type=F32) + bglu_ref[...])
    y_lru = (hl_ref[0].astype(F32) + hl_ref[1].astype(F32)) * _gelu(gl_ref[...].astype(F32))
    out = jnp.dot(y.astype(BF16), wout_ref[:D_S5, :], preferred_element_type=F32)
    out = out + jnp.dot(y_lru.astype(BF16), wout_ref[D_S5:, :], preferred_element_type=F32)
    g1 = g1_ref[...]
    zsum = DEEPNORM_ALPHA * xs_ref[...] + _per_batch(out, lambda o: o * g1[None])
    xs_new = _ln(zsum) * lng_ref[...] + lnb_ref[...]
    xo_ref[...] = xs_new
    h2_ref[...] = _modulate(xs_new, sh2_ref[...], sc2_ref[...]).astype(BF16)


def _mix_out_call(ys, u, hl, gl, xs, s5_d, w_glu, b_glu, w_out, g1, ln_g, ln_b, sh2, sc2):
    n = xs.shape[0]
    tm = 1024
    full = lambda shape: pl.BlockSpec(shape, lambda i: (0,) * len(shape))
    half = pl.BlockSpec((tm, D_S5), lambda i: (i, 0))
    both = pl.BlockSpec((2, tm, D_S5), lambda i: (0, i, 0))
    row = pl.BlockSpec((tm, D_MODEL), lambda i: (i, 0))
    return pl.pallas_call(
        _mix_out_kernel,
        grid=(n // tm,),
        in_specs=[both, half, both, half, row, full((1, D_S5)), full((D_S5, D_S5)), full((1, D_S5)),
                  full((D_MODEL, D_MODEL)), full((BATCH, D_MODEL)), full((1, D_MODEL)), full((1, D_MODEL)),
                  full((BATCH, D_MODEL)), full((BATCH, D_MODEL))],
        out_specs=[row, row],
        out_shape=[jax.ShapeDtypeStruct((n, D_MODEL), F32), jax.ShapeDtypeStruct((n, D_MODEL), BF16)],
        compiler_params=_cparams(("parallel",)),
        name="mix_out",
    )(ys, u, hl, gl, xs, s5_d, w_glu, b_glu, w_out, g1, ln_g, ln_b, sh2, sc2)


def _ffn_up_kernel(h_ref, w_ref, o_ref):
    o_ref[0] = jnp.dot(h_ref[...], w_ref[0], preferred_element_type=F32).astype(BF16)


def _ffn_up_call(h2, w_up):
    n = h2.shape[0]
    tm = 1024
    return pl.pallas_call(
        _ffn_up_kernel,
        grid=(2, n // tm),
        in_specs=[pl.BlockSpec((tm, D_MODEL), lambda s, i: (i, 0)),
                  pl.BlockSpec((1, D_MODEL, D_FF), lambda s, i: (s, 0, 0))],
        out_specs=pl.BlockSpec((1, tm, D_FF), lambda s, i: (s, i, 0)),
        out_shape=jax.ShapeDtypeStruct((2, n, D_FF), BF16),
        compiler_params=_cparams(("parallel", "parallel")),
        name="ffn_up",
    )(h2, w_up)


def _ffn_tail(c, n_chunks, act, wd_ref, xs_ref, g2_ref, lng_ref, lnb_ref, xo_ref, acc_ref):
    part = jnp.dot(act.astype(BF16), wd_ref[...], preferred_element_type=F32)

    @pl.when(c == 0)
    def _():
        acc_ref[...] = part

    @pl.when(c > 0)
    def _():
        acc_ref[...] += part

    @pl.when(c == n_chunks - 1)
    def _():
        g2 = g2_ref[...]
        zsum = DEEPNORM_ALPHA * xs_ref[...] + _per_batch(acc_ref[...], lambda o: o * g2[None])
        xo_ref[...] = _ln(zsum) * lng_ref[...] + lnb_ref[...]


GRID_ROW = GRID_W * BATCH


def _ffn_down_kernel(up_ref, uc_ref, un_ref, v_ref, cw_ref, cb_ref, wd_ref, xs_ref, g2_ref, lng_ref, lnb_ref,
                     xo_ref, upad_ref, acc_ref, *, n_tiles, n_chunks, halo, on_grid):
    i = pl.program_id(0)
    c = pl.program_id(1)
    tm = GRID_ROW
    base = BATCH + halo
    zeros = jnp.zeros((BATCH, FF_CHUNK), F32)
    upad_ref[0:BATCH, :] = zeros
    upad_ref[base + tm + halo:, :] = zeros
    upad_ref[BATCH:base, :] = jnp.where(i > 0, up_ref[0].astype(F32), 0.0)
    upad_ref[base:base + tm, :] = uc_ref[0].astype(F32)
    upad_ref[base + tm:base + tm + halo, :] = jnp.where(i < n_tiles - 1, un_ref[0].astype(F32), 0.0)

    row = lax.broadcasted_iota(jnp.int32, (tm, 1), 0)
    conv = cb_ref[...]
    for dx in (-1, 0, 1):
        part = None
        for dy in ((-1, 0, 1) if on_grid else (0,)):
            off = base + halo * dy + BATCH * dx
            tap = 3 * (dy + 1) + (dx + 1)
            term = cw_ref[tap:tap + 1, :] * upad_ref[off:off + tm, :]
            part = term if part is None else part + term
        if on_grid and dx == -1:
            part = jnp.where(row >= BATCH, part, 0.0)
        elif on_grid and dx == 1:
            part = jnp.where(row < tm - BATCH, part, 0.0)
        conv = conv + part
    act = _gelu(conv) * v_ref[0].astype(F32)
    _ffn_tail(c, n_chunks, act, wd_ref, xs_ref, g2_ref, lng_ref, lnb_ref, xo_ref, acc_ref)


def _ffn_down_call(uv, xs, conv_w, conv_b, w_down, g2, ln_g, ln_b, on_grid):
    n = xs.shape[0]
    tm = GRID_ROW
    n_tiles = n // tm
    n_chunks = D_FF // FF_CHUNK
    halo = GRID_ROW if on_grid else BATCH
    per_tile = tm // halo
    ublk = lambda rows, fn: pl.BlockSpec((1, rows, FF_CHUNK), fn)
    full = lambda shape: pl.BlockSpec(shape, lambda i, c: (0,) * len(shape))
    row = pl.BlockSpec((tm, D_MODEL), lambda i, c: (i, 0))
    return pl.pallas_call(
        functools.partial(_ffn_down_kernel, n_tiles=n_tiles, n_chunks=n_chunks, halo=halo, on_grid=on_grid),
        grid=(n_tiles, n_chunks),
        in_specs=[ublk(halo, lambda i, c: (0, jnp.maximum(i * per_tile - 1, 0), c)),
                  ublk(tm, lambda i, c: (0, i, c)),
                  ublk(halo, lambda i, c: (0, jnp.minimum((i + 1) * per_tile, n // halo - 1), c)),
                  ublk(tm, lambda i, c: (1, i, c)),
                  pl.BlockSpec((9, FF_CHUNK), lambda i, c: (0, c)),
                  pl.BlockSpec((1, FF_CHUNK), lambda i, c: (0, c)),
                  pl.BlockSpec((FF_CHUNK, D_MODEL), lambda i, c: (c, 0)),
                  row, full((BATCH, D_MODEL)), full((1, D_MODEL)), full((1, D_MODEL))],
        out_specs=row,
        out_shape=jax.ShapeDtypeStruct((n, D_MODEL), F32),
        scratch_shapes=[pltpu.VMEM((tm + 2 * halo + 2 * BATCH, FF_CHUNK), F32),
                        pltpu.VMEM((tm, D_MODEL), F32)],
        compiler_params=_cparams(("parallel", "arbitrary")),
        name="ffn_down_grid" if on_grid else "ffn_down_seq",
    )(uv, uv, uv, uv, conv_w, conv_b, w_down, xs, g2, ln_g, ln_b)


def _time_major(x):
    b, length, d = x.shape
    return x.transpose(1, 0, 2).reshape(length * b, d)


def kernel(x, c, ctx, c_ctx, w_mod, b_mod, w_in, s5_lam_re, s5_lam_im, s5_log_dt, s5_b_re, s5_b_im, s5_c_re, s5_c_im, s5_d, s5_w_glu, s5_b_glu, lru_conv_w, lru_conv_b, lru_w_a, lru_b_a, lru_w_x, lru_b_x, lru_lam, w_out, ln1_g, ln1_b, ffn_w_up, ffn_conv_w, ffn_conv_b, ffn_w_down, ln2_g, ln2_b):
    bsz, seq, _ = x.shape
    xs = _time_major(x)
    cs = _time_major(ctx)

    c_all = jnp.zeros((MOD_ROWS, D_MODEL), F32).at[:bsz].set(c).at[bsz].set(c_ctx)
    mods = _mod_call(c_all, w_mod, b_mod)

    eye_h = jnp.eye(LRU_HEADS, dtype=F32)
    s5_zero = jnp.zeros((2, S5_LANE_BLOCKS, BATCH, S5_STATE_COLS), F32)
    lru_zero = jnp.zeros((2, BATCH, D_LRU), F32)

    for l in range(DEPTH):
        last = l == DEPTH - 1
        lat = [mods[l, :bsz, k * D_MODEL:(k + 1) * D_MODEL] for k in range(N_MOD)]
        con = [jnp.broadcast_to(mods[l, bsz:bsz + 1, k * D_MODEL:(k + 1) * D_MODEL], (BATCH, D_MODEL))
               for k in range(N_MOD)]

        w_in_l = w_in[l].astype(BF16)
        mats = _s5_matrices(s5_lam_re[l], s5_lam_im[l], s5_log_dt[l], s5_b_re[l], s5_b_im[l],
                            s5_c_re[l], s5_c_im[l])
        dense = lambda wh: jnp.einsum("dhij,hk->dhikj", wh, eye_h).reshape(2, D_LRU, D_LRU)
        wg = jnp.concatenate([dense(lru_w_a[l]), dense(lru_w_x[l])], axis=-1).astype(BF16)
        bg = jnp.concatenate([lru_b_a[l], lru_b_x[l]], axis=-1).reshape(2, 1, 2 * D_LRU)
        lam = lru_lam[l].reshape(2, 1, D_LRU)
        conv_b = lru_conv_b[l].reshape(1, D_LRU)
        d_skip = s5_d[l].reshape(1, D_S5)
        w_glu = s5_w_glu[l].astype(BF16)
        b_glu = s5_b_glu[l].reshape(1, D_S5)
        w_out_l = w_out[l].astype(BF16)
        w_up = ffn_w_up[l].reshape(D_MODEL, 2, D_FF).transpose(1, 0, 2).astype(BF16)
        f_conv_w = ffn_conv_w[l].reshape(9, D_FF)
        f_conv_b = ffn_conv_b[l].reshape(1, D_FF)
        w_down = ffn_w_down[l].astype(BF16)
        ln1 = (ln1_g[l].reshape(1, D_MODEL), ln1_b[l].reshape(1, D_MODEL))
        ln2 = (ln2_g[l].reshape(1, D_MODEL), ln2_b[l].reshape(1, D_MODEL))

        def mixers(stream, mod, s5_init, lru_init):
            u, xl, gl = _mix_in_call(stream, mod[0], mod[1], w_in_l)
            ys, s5_fin = _s5_call(u, mats, s5_init)
            hl, lru_fin = _lru_call(xl, lru_conv_w[l], conv_b, wg, bg, lam, lru_init)
            return (ys, u, hl, gl), (s5_fin, lru_fin)

        def finish(stream, mixed, mod, on_grid):
            xs_mid, h2 = _mix_out_call(*mixed, stream, d_skip, w_glu, b_glu, w_out_l, mod[2], *ln1, mod[3], mod[4])
            uv = _ffn_up_call(h2, w_up)
            return _ffn_down_call(uv, xs_mid, f_conv_w, f_conv_b, w_down, mod[5], *ln2, on_grid)

        mixed_c, (s5_fin, lru_fin) = mixers(cs, con, s5_zero, lru_zero)
        mixed_x, _ = mixers(xs, lat, s5_fin, lru_fin)
        xs = finish(xs, mixed_x, lat, True)
        if not last:
            cs = finish(cs, mixed_c, con, False)

    return xs.reshape(seq, bsz, D_MODEL).transpose(1, 0, 2)
```

```python
import functools
import math

import jax
import jax.numpy as jnp
from jax import lax
from jax.experimental import pallas as pl
from jax.experimental.pallas import tpu as pltpu

F32 = jnp.float32
BF16 = jnp.bfloat16

D_MODEL = 1024
BATCH = 16
DEPTH = 4
GRID_W = 64
D_S5 = 512
D_LRU = 512
S5_GROUP_CH = 16
S5_GROUPS = 32
S5_STATE = 64
LRU_HEADS = 8
LRU_HEAD_DIM = 64
LRU_C = 8.0
D_FF = 2816
N_MOD = 6
LN_EPS = 1e-6
DEEPNORM_ALPHA = (2 * DEPTH) ** 0.25

LANES = 128
GROUPS_PER_LANE_BLOCK = LANES // S5_GROUP_CH
S5_LANE_BLOCKS = D_S5 // LANES
S5_STATE_COLS = 2 * GROUPS_PER_LANE_BLOCK * S5_STATE
S5_CHUNK = 8
S5_TK = S5_CHUNK * LANES
MOD_ROWS = 24
FF_CHUNK = 1408
VMEM_LIMIT = 56 * 1024 * 1024


def _cparams(sem):
    return pltpu.CompilerParams(dimension_semantics=sem, vmem_limit_bytes=VMEM_LIMIT)


def _sigmoid(x):
    return 0.5 * jnp.tanh(0.5 * x) + 0.5


def _gelu(x):
    return 0.5 * x * (1.0 + jnp.tanh(0.7978845608028654 * (x + 0.044715 * (x * x * x))))


def _ln(x):
    mu = jnp.mean(x, axis=-1, keepdims=True)
    xc = x - mu
    var = jnp.mean(xc * xc, axis=-1, keepdims=True)
    return xc * lax.rsqrt(var + LN_EPS)


def _per_batch(x, fn):
    rows, d = x.shape
    return fn(x.reshape(rows // BATCH, BATCH, d)).reshape(rows, d)


def _modulate(x, shift, scale):
    return _per_batch(_ln(x), lambda h: h * (1.0 + scale)[None] + shift[None])


def _mod_kernel(c_ref, w_ref, b_ref, o_ref):
    c = c_ref[...]
    s = c * _sigmoid(c)
    o_ref[0] = jnp.dot(s, w_ref[0], preferred_element_type=F32) + b_ref[0]


def _mod_call(c_all, w_mod, b_mod):
    tn = 1536
    return pl.pallas_call(
        _mod_kernel,
        grid=(DEPTH, N_MOD * D_MODEL // tn),
        in_specs=[pl.BlockSpec((MOD_ROWS, D_MODEL), lambda l, j: (0, 0)),
                  pl.BlockSpec((1, D_MODEL, tn), lambda l, j: (l, 0, j)),
                  pl.BlockSpec((1, 1, tn), lambda l, j: (l, 0, j))],
        out_specs=pl.BlockSpec((1, MOD_ROWS, tn), lambda l, j: (l, 0, j)),
        out_shape=jax.ShapeDtypeStruct((DEPTH, MOD_ROWS, N_MOD * D_MODEL), F32),
        compiler_params=_cparams(("parallel", "parallel")),
        name="mod",
    )(c_all, w_mod, b_mod.reshape(DEPTH, 1, N_MOD * D_MODEL))


def _mix_in_kernel(x_ref, sh_ref, sc_ref, w_ref, u_ref, xl_ref, gl_ref):
    h = _modulate(x_ref[...], sh_ref[...], sc_ref[...]).astype(BF16)
    p = jnp.dot(h, w_ref[...], preferred_element_type=F32)
    u_ref[...] = p[:, :D_S5].astype(BF16)
    xl_ref[...] = p[:, D_S5:D_S5 + D_LRU].astype(BF16)
    gl_ref[...] = p[:, D_S5 + D_LRU:].astype(BF16)


def _mix_in_call(xs, shift, scale, w_in):
    n = xs.shape[0]
    tm = 1024
    vec = pl.BlockSpec((BATCH, D_MODEL), lambda i: (0, 0))
    out = pl.BlockSpec((tm, D_S5), lambda i: (i, 0))
    return pl.pallas_call(
        _mix_in_kernel,
        grid=(n // tm,),
        in_specs=[pl.BlockSpec((tm, D_MODEL), lambda i: (i, 0)), vec, vec,
                  pl.BlockSpec((D_MODEL, D_S5 + 2 * D_LRU), lambda i: (0, 0))],
        out_specs=[out, out, out],
        out_shape=[jax.ShapeDtypeStruct((n, D_S5), BF16)] * 3,
        compiler_params=_cparams(("parallel",)),
        name="mix_in",
    )(xs, shift, scale, w_in)


def _s5_matrices(lam_re, lam_im, log_dt, b_re, b_im, c_re, c_im):
    T = S5_CHUNK
    lam = lax.complex(lam_re, lam_im)
    lam_dt = lam * jnp.exp(log_dt)[..., None]
    lam_bar = jnp.exp(lam_dt)
    b_bar = ((lam_bar - 1.0) / lam)[..., None] * lax.complex(b_re, b_im)
    c_mat = lax.complex(c_re, c_im)
    steps = jnp.arange(T + 1, dtype=F32)
    pw = jnp.exp(steps[:, None, None, None] * lam_dt[None])
    kd = jnp.real(jnp.einsum("dgop,ndgp,dgpi->ndgoi", c_mat, pw[:T], b_bar))
    eye = jnp.eye(GROUPS_PER_LANE_BLOCK, dtype=F32)
    nb, gpb = S5_LANE_BLOCKS, GROUPS_PER_LANE_BLOCK

    ar = jnp.arange(T)
    kb = jnp.einsum("ndjgoi,gh->dnjgiho", kd.reshape(T, 2, nb, gpb, S5_GROUP_CH, S5_GROUP_CH), eye)
    kb = kb.reshape(2, T, nb, LANES, LANES).astype(BF16)
    zero = jnp.zeros((nb, LANES, LANES), BF16)
    m = jnp.stack([
        jnp.stack([jnp.concatenate(
            [kb[d, (b - a) * (1 - 2 * d)] if (b - a) * (1 - 2 * d) >= 0 else zero for b in range(T)], axis=-1)
            for a in range(T)], axis=1)
        for d in range(2)]).reshape(2, nb, S5_TK, S5_TK)

    e_in = jnp.stack([T - 1 - ar, ar])
    wc = jnp.stack([pw[e_in[d], d] for d in range(2)])[..., None] * b_bar[:, None]
    wri = jnp.stack([jnp.real(wc), jnp.imag(wc)]).reshape(2, 2, T, nb, gpb, S5_STATE, S5_GROUP_CH)
    w = jnp.einsum("rdtjgpi,gh->djtgirhp", wri, eye).reshape(2, nb, S5_TK, S5_STATE_COLS)

    f_out = jnp.stack([ar + 1, T - ar])
    zc = c_mat[:, None] * jnp.stack([pw[f_out[d], d] for d in range(2)])[:, :, :, None, :]
    zri = jnp.stack([jnp.real(zc), -jnp.imag(zc)]).reshape(2, 2, T, nb, gpb, S5_GROUP_CH, S5_STATE)
    z = jnp.einsum("rdtjgop,gh->djrgptho", zri, eye).reshape(2, nb, S5_STATE_COLS, S5_TK)

    a_t = pw[T].reshape(2, nb, gpb * S5_STATE)
    a = jnp.stack([jnp.real(a_t), jnp.imag(a_t)], axis=2)
    return m.astype(BF16), w.astype(BF16), z.astype(BF16), a


def _s5_kernel(u_ref, m_ref, w_ref, z_ref, a_ref, init_ref, y_ref, fin_ref, s_ref, v_ref, sp_ref, *, kc, n_tiles):
    d = pl.program_id(0)
    rt = pl.program_id(2)
    half = S5_STATE_COLS // 2

    @pl.when(rt == 0)
    def _():
        s_ref[...] = init_ref[0, 0]

    ucat = jnp.concatenate(
        [u_ref[:, BATCH * t:BATCH * (t + 1), :].reshape(kc * BATCH, LANES) for t in range(S5_CHUNK)], axis=1)
    v_ref[...] = jnp.dot(ucat, w_ref[0, 0], preferred_element_type=F32)

    a_re = jnp.broadcast_to(a_ref[0, 0, 0:1, :], (BATCH, half))
    a_im = jnp.broadcast_to(a_ref[0, 0, 1:2, :], (BATCH, half))

    def step(i, carry):
        s_re, s_im = carry
        k = i + d * (kc - 1 - 2 * i)
        off = pl.multiple_of(k * BATCH, BATCH)
        sp_ref[pl.ds(off, BATCH), :half] = s_re
        sp_ref[pl.ds(off, BATCH), half:] = s_im
        v = v_ref[pl.ds(off, BATCH), :]
        n_re = a_re * s_re - a_im * s_im + v[:, :half]
        n_im = a_re * s_im + a_im * s_re + v[:, half:]
        return n_re, n_im

    s_re, s_im = lax.fori_loop(0, kc, step, (s_ref[:, :half], s_ref[:, half:]))
    s_ref[:, :half] = s_re
    s_ref[:, half:] = s_im

    y = jnp.dot(ucat, m_ref[0, 0], preferred_element_type=F32)
    y = y + jnp.dot(sp_ref[...].astype(BF16), z_ref[0, 0], preferred_element_type=F32)
    for t in range(S5_CHUNK):
        y_ref[0, :, BATCH * t:BATCH * (t + 1), :] = (
            y[:, LANES * t:LANES * (t + 1)].reshape(kc, BATCH, LANES).astype(BF16))

    @pl.when(rt == n_tiles - 1)
    def _():
        fin_ref[0, 0] = s_ref[...]


def _s5_call(u, mats, init):
    m, w, z, a = mats
    n = u.shape[0]
    chunk_rows = S5_CHUNK * BATCH
    n_chunks = n // chunk_rows
    kc = min(64, n_chunks)
    n_tiles = n_chunks // kc
    nb = S5_LANE_BLOCKS

    def tile(d, rt):
        return rt + d * (n_tiles - 1 - 2 * rt)

    per_dir = lambda shape: pl.BlockSpec((1, 1) + shape, lambda d, j, rt: (d, j, 0, 0))
    y, fin = pl.pallas_call(
        functools.partial(_s5_kernel, kc=kc, n_tiles=n_tiles),
        grid=(2, nb, n_tiles),
        in_specs=[pl.BlockSpec((kc, chunk_rows, LANES), lambda d, j, rt: (tile(d, rt), 0, j)),
                  per_dir((S5_TK, S5_TK)), per_dir((S5_TK, S5_STATE_COLS)), per_dir((S5_STATE_COLS, S5_TK)),
                  per_dir((2, S5_STATE_COLS // 2)), per_dir((BATCH, S5_STATE_COLS))],
        out_specs=[pl.BlockSpec((1, kc, chunk_rows, LANES), lambda d, j, rt: (d, tile(d, rt), 0, j)),
                   per_dir((BATCH, S5_STATE_COLS))],
        out_shape=[jax.ShapeDtypeStruct((2, n_chunks, chunk_rows, D_S5), BF16),
                   jax.ShapeDtypeStruct((2, nb, BATCH, S5_STATE_COLS), F32)],
        scratch_shapes=[pltpu.VMEM((BATCH, S5_STATE_COLS), F32),
                        pltpu.VMEM((kc * BATCH, S5_STATE_COLS), F32),
                        pltpu.VMEM((kc * BATCH, S5_STATE_COLS), F32)],
        compiler_params=_cparams(("arbitrary", "arbitrary", "arbitrary")),
        name="s5",
    )(u.reshape(n_chunks, chunk_rows, D_S5), m, w, z, a, init)
    return y.reshape(2, n, D_S5), fin


LRU_HALO_PREV = 2 * BATCH
LRU_HALO_NEXT = BATCH


def _lru_kernel(xp_ref, xc_ref, xn_ref, cw_ref, cb_ref, wg_ref, bg_ref, lam_ref, init_ref,
                h_ref, fin_ref, xpad_ref, a_ref, b_ref, hst_ref, *, tm, n_tiles):
    d = pl.program_id(0)
    i = pl.program_id(1)
    te = i + d * (n_tiles - 1 - 2 * i)

    @pl.when(i == 0)
    def _():
        hst_ref[...] = init_ref[0]

    xpad_ref[0:LRU_HALO_PREV, :] = jnp.where(te > 0, xp_ref[...].astype(F32), 0.0)
    xpad_ref[LRU_HALO_PREV:LRU_HALO_PREV + tm, :] = xc_ref[...].astype(F32)
    xpad_ref[LRU_HALO_PREV + tm:, :] = jnp.where(te < n_tiles - 1, xn_ref[...].astype(F32), 0.0)

    xc = cb_ref[...]
    for k in range(4):
        xc = xc + cw_ref[k:k + 1, :] * xpad_ref[BATCH * k:BATCH * k + tm, :]

    gates = jnp.dot(xc.astype(BF16), wg_ref[0], preferred_element_type=F32) + bg_ref[0]
    r = _sigmoid(gates[:, :D_LRU])
    ig = _sigmoid(gates[:, D_LRU:])
    z = -lam_ref[0]
    softplus = jnp.maximum(z, 0.0) + jnp.log1p(jnp.exp(-jnp.abs(z)))
    log_a = (-LRU_C * r) * softplus
    a = jnp.exp(log_a)
    a_ref[...] = a
    b_ref[...] = jnp.sqrt(-jnp.tanh(log_a) * (a * a + 1.0)) * (ig * xc)

    steps = tm // BATCH

    def step(j, h):
        k = j + d * (steps - 1 - 2 * j)
        off = pl.multiple_of(k * BATCH, BATCH)
        h = a_ref[pl.ds(off, BATCH), :] * h + b_ref[pl.ds(off, BATCH), :]
        b_ref[pl.ds(off, BATCH), :] = h
        return h

    h = lax.fori_loop(0, steps, step, hst_ref[...])
    hst_ref[...] = h
    h_ref[0] = b_ref[...].astype(BF16)

    @pl.when(i == n_tiles - 1)
    def _():
        fin_ref[0] = h


def _lru_call(xl, conv_w, conv_b, wg, bg, lam, init):
    n = xl.shape[0]
    tm = 2048
    n_tiles = n // tm

    def tile(d, i):
        return i + d * (n_tiles - 1 - 2 * i)

    prev_blocks = tm // LRU_HALO_PREV
    next_blocks = tm // LRU_HALO_NEXT
    full = lambda shape: pl.BlockSpec(shape, lambda d, i: (0,) * len(shape))
    per_dir = lambda shape: pl.BlockSpec((1,) + shape, lambda d, i: (d,) + (0,) * len(shape))
    return pl.pallas_call(
        functools.partial(_lru_kernel, tm=tm, n_tiles=n_tiles),
        grid=(2, n_tiles),
        in_specs=[pl.BlockSpec((LRU_HALO_PREV, D_LRU),
                               lambda d, i: (jnp.maximum(tile(d, i) * prev_blocks - 1, 0), 0)),
                  pl.BlockSpec((tm, D_LRU), lambda d, i: (tile(d, i), 0)),
                  pl.BlockSpec((LRU_HALO_NEXT, D_LRU),
                               lambda d, i: (jnp.minimum((tile(d, i) + 1) * next_blocks, n // LRU_HALO_NEXT - 1), 0)),
                  full((4, D_LRU)), full((1, D_LRU)),
                  per_dir((D_LRU, 2 * D_LRU)), per_dir((1, 2 * D_LRU)), per_dir((1, D_LRU)),
                  per_dir((BATCH, D_LRU))],
        out_specs=[pl.BlockSpec((1, tm, D_LRU), lambda d, i: (d, tile(d, i), 0)),
                   per_dir((BATCH, D_LRU))],
        out_shape=[jax.ShapeDtypeStruct((2, n, D_LRU), BF16),
                   jax.ShapeDtypeStruct((2, BATCH, D_LRU), F32)],
        scratch_shapes=[pltpu.VMEM((LRU_HALO_PREV + tm + LRU_HALO_NEXT, D_LRU), F32),
                        pltpu.VMEM((tm, D_LRU), F32),
                        pltpu.VMEM((tm, D_LRU), F32),
                        pltpu.VMEM((BATCH, D_LRU), F32)],
        compiler_params=_cparams(("arbitrary", "arbitrary")),
        name="lru",
    )(xl, xl, xl, conv_w, conv_b, wg, bg, lam, init)


def _mix_out_kernel(ys_ref, u_ref, hl_ref, gl_ref, xs_ref, d_ref, wglu_ref, bglu_ref, wout_ref,
                    g1_ref, lng_ref, lnb_ref, sh2_ref, sc2_ref, xo_ref, h2_ref):
    y = ys_ref[0].astype(F32) + ys_ref[1].astype(F32) + d_ref[...] * u_ref[...].astype(F32)
    y = _gelu(y)
    y = y * _sigmoid(jnp.dot(y.astype(BF16), wglu_ref[...], preferred_element_type=F32) + bglu_ref[...])
    y_lru = (hl_ref[0].astype(F32) + hl_ref[1].astype(F32)) * _gelu(gl_ref[...].astype(F32))
    out = jnp.dot(y.astype(BF16), wout_ref[:D_S5, :], preferred_element_type=F32)
    out = out + jnp.dot(y_lru.astype(BF16), wout_ref[D_S5:, :], preferred_element_type=F32)
    g1 = g1_ref[...]
    zsum = DEEPNORM_ALPHA * xs_ref[...] + _per_batch(out, lambda o: o * g1[None])
    xs_new = _ln(zsum) * lng_ref[...] + lnb_ref[...]
    xo_ref[...] = xs_new
    h2_ref[...] = _modulate(xs_new, sh2_ref[...], sc2_ref[...]).astype(BF16)


def _mix_out_call(ys, u, hl, gl, xs, s5_d, w_glu, b_glu, w_out, g1, ln_g, ln_b, sh2, sc2):
    n = xs.shape[0]
    tm = 1024
    full = lambda shape: pl.BlockSpec(shape, lambda i: (0,) * len(shape))
    half = pl.BlockSpec((tm, D_S5), lambda i: (i, 0))
    both = pl.BlockSpec((2, tm, D_S5), lambda i: (0, i, 0))
    row = pl.BlockSpec((tm, D_MODEL), lambda i: (i, 0))
    return pl.pallas_call(
        _mix_out_kernel,
        grid=(n // tm,),
        in_specs=[both, half, both, half, row, full((1, D_S5)), full((D_S5, D_S5)), full((1, D_S5)),
                  full((D_MODEL, D_MODEL)), full((BATCH, D_MODEL)), full((1, D_MODEL)), full((1, D_MODEL)),
                  full((BATCH, D_MODEL)), full((BATCH, D_MODEL))],
        out_specs=[row, row],
        out_shape=[jax.ShapeDtypeStruct((n, D_MODEL), F32), jax.ShapeDtypeStruct((n, D_MODEL), BF16)],
        compiler_params=_cparams(("parallel",)),
        name="mix_out",
    )(ys, u, hl, gl, xs, s5_d, w_glu, b_glu, w_out, g1, ln_g, ln_b, sh2, sc2)


def _ffn_up_kernel(h_ref, w_ref, o_ref):
    o_ref[0] = jnp.dot(h_ref[...], w_ref[0], preferred_element_type=F32).astype(BF16)


def _ffn_up_call(h2, w_up):
    n = h2.shape[0]
    tm = 1024
    return pl.pallas_call(
        _ffn_up_kernel,
        grid=(2, n // tm),
        in_specs=[pl.BlockSpec((tm, D_MODEL), lambda s, i: (i, 0)),
                  pl.BlockSpec((1, D_MODEL, D_FF), lambda s, i: (s, 0, 0))],
        out_specs=pl.BlockSpec((1, tm, D_FF), lambda s, i: (s, i, 0)),
        out_shape=jax.ShapeDtypeStruct((2, n, D_FF), BF16),
        compiler_params=_cparams(("parallel", "parallel")),
        name="ffn_up",
    )(h2, w_up)


GRID_ROW = GRID_W * BATCH
EPILOGUE_ROWS = 256


def _ffn_down_kernel(u_ref, v_ref, cw_ref, cb_ref, wd_ref, xs_ref, g2_ref, lng_ref, lnb_ref,
                     xo_ref, win_ref, act_ref, *, n_tiles, n_chunks, on_grid):
    s = pl.program_id(0)
    c = pl.program_id(1)
    tm = GRID_ROW
    win = win_ref.at[c]

    @pl.when(s == 0)
    def _():
        win[0] = jnp.zeros((tm, FF_CHUNK), BF16)

    @pl.when(s > 0)
    def _():
        edge = jnp.zeros((BATCH, LANES), BF16)
        for k in range(FF_CHUNK // LANES):
            ls = slice(LANES * k, LANES * (k + 1))
            cur = win[1, :, ls]
            nxt = jnp.where(s < n_tiles, u_ref[0, :, ls], jnp.zeros((), BF16))
            w = [cw_ref[t:t + 1, ls].astype(BF16) for t in range(9)]
            if on_grid:
                rows = ((-1, win[0, :, ls], edge, edge), (0, cur, edge, edge), (1, nxt, edge, edge))
            else:
                rows = ((0, cur, win[0, tm - BATCH:, ls], nxt[:BATCH]),)
            conv = cb_ref[:, ls].astype(BF16)
            for dy, mid, before, after in rows:
                left = jnp.concatenate([before, mid[:tm - BATCH]], axis=0)
                right = jnp.concatenate([mid[BATCH:], after], axis=0)
                t0 = 3 * (dy + 1)
                conv = conv + (w[t0] * left + w[t0 + 1] * mid + w[t0 + 2] * right)
            act_ref[:, ls] = _gelu(conv) * v_ref[0, :, ls]
        part = jnp.dot(act_ref[...], wd_ref[...], preferred_element_type=F32)

        @pl.when(c == 0)
        def _():
            xo_ref[...] = part

        @pl.when(c > 0)
        def _():
            xo_ref[...] += part

        @pl.when(c == n_chunks - 1)
        def _():
            g2 = g2_ref[...]
            for r in range(tm // EPILOGUE_ROWS):
                rs = slice(EPILOGUE_ROWS * r, EPILOGUE_ROWS * (r + 1))
                zsum = DEEPNORM_ALPHA * xs_ref[rs, :] + _per_batch(xo_ref[rs, :], lambda o: o * g2[None])
                xo_ref[rs, :] = _ln(zsum) * lng_ref[...] + lnb_ref[...]

        win[0] = win[1]

    win[1] = u_ref[0]


def _ffn_down_call(uv, xs, conv_w, conv_b, w_down, g2, ln_g, ln_b, on_grid):
    n = xs.shape[0]
    tm = GRID_ROW
    n_tiles = n // tm
    n_chunks = D_FF // FF_CHUNK
    full = lambda shape: pl.BlockSpec(shape, lambda s, c: (0,) * len(shape))
    done = lambda s: jnp.maximum(s - 1, 0)
    row = pl.BlockSpec((tm, D_MODEL), lambda s, c: (done(s), 0))
    return pl.pallas_call(
        functools.partial(_ffn_down_kernel, n_tiles=n_tiles, n_chunks=n_chunks, on_grid=on_grid),
        grid=(n_tiles + 1, n_chunks),
        in_specs=[pl.BlockSpec((1, tm, FF_CHUNK), lambda s, c: (0, jnp.minimum(s, n_tiles - 1), c)),
                  pl.BlockSpec((1, tm, FF_CHUNK), lambda s, c: (1, done(s), c)),
                  pl.BlockSpec((9, FF_CHUNK), lambda s, c: (0, c)),
                  pl.BlockSpec((1, FF_CHUNK), lambda s, c: (0, c)),
                  pl.BlockSpec((FF_CHUNK, D_MODEL), lambda s, c: (c, 0)),
                  row, full((BATCH, D_MODEL)), full((1, D_MODEL)), full((1, D_MODEL))],
        out_specs=row,
        out_shape=jax.ShapeDtypeStruct((n, D_MODEL), F32),
        scratch_shapes=[pltpu.VMEM((n_chunks, 2, tm, FF_CHUNK), BF16),
                        pltpu.VMEM((tm, FF_CHUNK), BF16)],
        compiler_params=_cparams(("arbitrary", "arbitrary")),
        name="ffn_down_grid" if on_grid else "ffn_down_seq",
    )(uv, uv, conv_w, conv_b, w_down, xs, g2, ln_g, ln_b)


def _time_major(x):
    b, length, d = x.shape
    return x.transpose(1, 0, 2).reshape(length * b, d)


def kernel(x, c, ctx, c_ctx, w_mod, b_mod, w_in, s5_lam_re, s5_lam_im, s5_log_dt, s5_b_re, s5_b_im, s5_c_re, s5_c_im, s5_d, s5_w_glu, s5_b_glu, lru_conv_w, lru_conv_b, lru_w_a, lru_b_a, lru_w_x, lru_b_x, lru_lam, w_out, ln1_g, ln1_b, ffn_w_up, ffn_conv_w, ffn_conv_b, ffn_w_down, ln2_g, ln2_b):
    bsz, seq, _ = x.shape
    xs = _time_major(x)
    cs = _time_major(ctx)

    c_all = jnp.zeros((MOD_ROWS, D_MODEL), F32).at[:bsz].set(c).at[bsz].set(c_ctx)
    mods = _mod_call(c_all, w_mod, b_mod)

    eye_h = jnp.eye(LRU_HEADS, dtype=F32)
    s5_zero = jnp.zeros((2, S5_LANE_BLOCKS, BATCH, S5_STATE_COLS), F32)
    lru_zero = jnp.zeros((2, BATCH, D_LRU), F32)

    for l in range(DEPTH):
        last = l == DEPTH - 1
        lat = [mods[l, :bsz, k * D_MODEL:(k + 1) * D_MODEL] for k in range(N_MOD)]
        con = [jnp.broadcast_to(mods[l, bsz:bsz + 1, k * D_MODEL:(k + 1) * D_MODEL], (BATCH, D_MODEL))
               for k in range(N_MOD)]

        w_in_l = w_in[l].astype(BF16)
        mats = _s5_matrices(s5_lam_re[l], s5_lam_im[l], s5_log_dt[l], s5_b_re[l], s5_b_im[l],
                            s5_c_re[l], s5_c_im[l])
        dense = lambda wh: jnp.einsum("dhij,hk->dhikj", wh, eye_h).reshape(2, D_LRU, D_LRU)
        wg = jnp.concatenate([dense(lru_w_a[l]), dense(lru_w_x[l])], axis=-1).astype(BF16)
        bg = jnp.concatenate([lru_b_a[l], lru_b_x[l]], axis=-1).reshape(2, 1, 2 * D_LRU)
        lam = lru_lam[l].reshape(2, 1, D_LRU)
        conv_b = lru_conv_b[l].reshape(1, D_LRU)
        d_skip = s5_d[l].reshape(1, D_S5)
        w_glu = s5_w_glu[l].astype(BF16)
        b_glu = s5_b_glu[l].reshape(1, D_S5)
        w_out_l = w_out[l].astype(BF16)
        w_up = ffn_w_up[l].reshape(D_MODEL, 2, D_FF).transpose(1, 0, 2).astype(BF16)
        f_conv_w = ffn_conv_w[l].reshape(9, D_FF)
        f_conv_b = ffn_conv_b[l].reshape(1, D_FF)
        w_down = ffn_w_down[l].astype(BF16)
        ln1 = (ln1_g[l].reshape(1, D_MODEL), ln1_b[l].reshape(1, D_MODEL))
        ln2 = (ln2_g[l].reshape(1, D_MODEL), ln2_b[l].reshape(1, D_MODEL))

        def mixers(stream, mod, s5_init, lru_init):
            u, xl, gl = _mix_in_call(stream, mod[0], mod[1], w_in_l)
            ys, s5_fin = _s5_call(u, mats, s5_init)
            hl, lru_fin = _lru_call(xl, lru_conv_w[l], conv_b, wg, bg, lam, lru_init)
            return (ys, u, hl, gl), (s5_fin, lru_fin)

        def finish(stream, mixed, mod, on_grid):
            xs_mid, h2 = _mix_out_call(*mixed, stream, d_skip, w_glu, b_glu, w_out_l, mod[2], *ln1, mod[3], mod[4])
            uv = _ffn_up_call(h2, w_up)
            return _ffn_down_call(uv, xs_mid, f_conv_w, f_conv_b, w_down, mod[5], *ln2, on_grid)

        mixed_c, (s5_fin, lru_fin) = mixers(cs, con, s5_zero, lru_zero)
        mixed_x, _ = mixers(xs, lat, s5_fin, lru_fin)
        xs = finish(xs, mixed_x, lat, True)
        if not last:
            cs = finish(cs, mixed_c, con, False)

    return xs.reshape(seq, bsz, D_MODEL).transpose(1, 0, 2)
```

```python
import functools
import math

import jax
import jax.numpy as jnp
from jax import lax
from jax.experimental import pallas as pl
from jax.experimental.pallas import tpu as pltpu

F32 = jnp.float32
BF16 = jnp.bfloat16

D_MODEL = 1024
BATCH = 16
DEPTH = 4
GRID_W = 64
D_S5 = 512
D_LRU = 512
S5_GROUP_CH = 16
S5_GROUPS = 32
S5_STATE = 64
LRU_HEADS = 8
LRU_HEAD_DIM = 64
LRU_C = 8.0
D_FF = 2816
N_MOD = 6
LN_EPS = 1e-6
DEEPNORM_ALPHA = (2 * DEPTH) ** 0.25

LANES = 128
GROUPS_PER_LANE_BLOCK = LANES // S5_GROUP_CH
S5_LANE_BLOCKS = D_S5 // LANES
S5_STATE_COLS = 2 * GROUPS_PER_LANE_BLOCK * S5_STATE
S5_CHUNK = 8
S5_TK = S5_CHUNK * LANES
MOD_ROWS = 24
FF_CHUNK = 1408
VMEM_LIMIT = 56 * 1024 * 1024


def _cparams(sem):
    return pltpu.CompilerParams(dimension_semantics=sem, vmem_limit_bytes=VMEM_LIMIT)


def _sigmoid(x):
    return 0.5 * jnp.tanh(0.5 * x) + 0.5


def _gelu(x):
    return 0.5 * x * (1.0 + jnp.tanh(0.7978845608028654 * (x + 0.044715 * (x * x * x))))


def _ln(x):
    mu = jnp.mean(x, axis=-1, keepdims=True)
    xc = x - mu
    var = jnp.mean(xc * xc, axis=-1, keepdims=True)
    return xc * lax.rsqrt(var + LN_EPS)


def _per_batch(x, fn):
    rows, d = x.shape
    return fn(x.reshape(rows // BATCH, BATCH, d)).reshape(rows, d)


def _modulate(x, shift, scale):
    return _per_batch(_ln(x), lambda h: h * (1.0 + scale)[None] + shift[None])


def _mod_kernel(c_ref, w_ref, b_ref, o_ref):
    c = c_ref[...]
    s = c * _sigmoid(c)
    o_ref[0] = jnp.dot(s, w_ref[0], preferred_element_type=F32) + b_ref[0]


def _mod_call(c_all, w_mod, b_mod):
    tn = 1536
    return pl.pallas_call(
        _mod_kernel,
        grid=(DEPTH, N_MOD * D_MODEL // tn),
        in_specs=[pl.BlockSpec((MOD_ROWS, D_MODEL), lambda l, j: (0, 0)),
                  pl.BlockSpec((1, D_MODEL, tn), lambda l, j: (l, 0, j)),
                  pl.BlockSpec((1, 1, tn), lambda l, j: (l, 0, j))],
        out_specs=pl.BlockSpec((1, MOD_ROWS, tn), lambda l, j: (l, 0, j)),
        out_shape=jax.ShapeDtypeStruct((DEPTH, MOD_ROWS, N_MOD * D_MODEL), F32),
        compiler_params=_cparams(("parallel", "parallel")),
        name="mod",
    )(c_all, w_mod, b_mod.reshape(DEPTH, 1, N_MOD * D_MODEL))


def _mix_in_kernel(x_ref, sh_ref, sc_ref, w_ref, u_ref, xl_ref, gl_ref):
    h = _modulate(x_ref[...], sh_ref[...], sc_ref[...]).astype(BF16)
    p = jnp.dot(h, w_ref[...], preferred_element_type=F32)
    u_ref[...] = p[:, :D_S5].astype(BF16)
    xl_ref[...] = p[:, D_S5:D_S5 + D_LRU].astype(BF16)
    gl_ref[...] = p[:, D_S5 + D_LRU:].astype(BF16)


def _mix_in_call(xs, shift, scale, w_in):
    n = xs.shape[0]
    tm = 1024
    vec = pl.BlockSpec((BATCH, D_MODEL), lambda i: (0, 0))
    out = pl.BlockSpec((tm, D_S5), lambda i: (i, 0))
    return pl.pallas_call(
        _mix_in_kernel,
        grid=(n // tm,),
        in_specs=[pl.BlockSpec((tm, D_MODEL), lambda i: (i, 0)), vec, vec,
                  pl.BlockSpec((D_MODEL, D_S5 + 2 * D_LRU), lambda i: (0, 0))],
        out_specs=[out, out, out],
        out_shape=[jax.ShapeDtypeStruct((n, D_S5), BF16)] * 3,
        compiler_params=_cparams(("parallel",)),
        name="mix_in",
    )(xs, shift, scale, w_in)


def _s5_matrices(lam_re, lam_im, log_dt, b_re, b_im, c_re, c_im):
    T = S5_CHUNK
    lam = lax.complex(lam_re, lam_im)
    lam_dt = lam * jnp.exp(log_dt)[..., None]
    lam_bar = jnp.exp(lam_dt)
    b_bar = ((lam_bar - 1.0) / lam)[..., None] * lax.complex(b_re, b_im)
    c_mat = lax.complex(c_re, c_im)
    steps = jnp.arange(T + 1, dtype=F32)
    pw = jnp.exp(steps[:, None, None, None] * lam_dt[None])
    kd = jnp.real(jnp.einsum("dgop,ndgp,dgpi->ndgio", c_mat, pw[:T], b_bar))
    nb, gpb = S5_LANE_BLOCKS, GROUPS_PER_LANE_BLOCK
    half = gpb * S5_STATE

    lane_group = jnp.arange(LANES) // S5_GROUP_CH
    state_group = jnp.arange(half) // S5_STATE
    rep_ch = jnp.tile(jnp.eye(S5_GROUP_CH, dtype=BF16), (1, gpb))
    rep_st = jnp.tile(jnp.eye(S5_STATE, dtype=BF16), (1, gpb))
    same_ll = (lane_group[:, None] == lane_group[None, :]).astype(BF16)
    same_ls = (lane_group[:, None] == state_group[None, :]).astype(BF16)
    same_sl = same_ls.T

    def spread(x, rep, same, out):
        return jnp.einsum(out, x.astype(BF16), rep, preferred_element_type=BF16) * same

    ar = jnp.arange(T)
    kd = kd.at[0, 0].add(kd[0, 1])
    kb = spread(kd.reshape(T, 2, nb, LANES, S5_GROUP_CH), rep_ch, same_ll, "ndjxo,oq->dnjxq")
    m = jnp.stack([jnp.concatenate([kb[0, b - a] if b >= a else kb[1, a - b] for b in range(T)], axis=-1)
                   for a in range(T)], axis=1).reshape(nb, S5_TK, S5_TK)

    e_in = jnp.stack([T - 1 - ar, ar])
    pw_in = jnp.stack([pw[e_in[d], d] for d in range(2)])
    wc = (pw_in[:, :, :, None, :] * jnp.swapaxes(b_bar, -1, -2)[:, None]).reshape(2, T, nb, LANES, S5_STATE)
    w = jnp.concatenate([spread(part, rep_st, same_ls, "dtjxp,pq->djtxq")
                         for part in (jnp.real(wc), jnp.imag(wc))], axis=-1)
    w = w.reshape(2, nb, S5_TK, S5_STATE_COLS)

    f_out = jnp.stack([ar + 1, T - ar])
    pw_out = jnp.stack([pw[f_out[d], d] for d in range(2)])
    zc = (jnp.swapaxes(c_mat, -1, -2)[:, None] * pw_out[..., None]).reshape(2, T, nb, half, S5_GROUP_CH)
    z = jnp.concatenate([
        jnp.concatenate([spread(part[:, t], rep_ch, same_sl, "djyo,oq->djyq") for t in range(T)], axis=-1)
        for part in (jnp.real(zc), -jnp.imag(zc))], axis=-2)

    a_t = pw[T].reshape(2, nb, half)
    a = jnp.stack([jnp.real(a_t), jnp.imag(a_t)], axis=2)
    return w, jnp.concatenate([m, z[0], z[1]], axis=1), a


def _s5_kernel(u_ref, w_ref, mz_ref, a_ref, init_ref, y_ref, fin_ref, s_ref, v_ref, spf_ref, spb_ref, *,
               kc, n_tiles):
    ph = pl.program_id(1)
    rt = pl.program_id(2)
    half = S5_STATE_COLS // 2
    rows = kc * BATCH

    @pl.when((ph == 0) & (rt == 0))
    def _():
        s_ref[...] = init_ref[:, 0]

    ucat = jnp.concatenate(
        [u_ref[:, BATCH * t:BATCH * (t + 1), :].reshape(rows, LANES) for t in range(S5_CHUNK)], axis=1)

    def recurrence(d, sp_ref, base):
        v_ref[...] = jnp.dot(ucat, w_ref[0, 0], preferred_element_type=F32)
        a_re = jnp.broadcast_to(a_ref[d, 0, 0:1, :], (BATCH, half))
        a_im = jnp.broadcast_to(a_ref[d, 0, 1:2, :], (BATCH, half))

        def step(i, carry):
            s_re, s_im = carry
            off = pl.multiple_of((kc - 1 - i if d else i) * BATCH, BATCH)
            dst = pl.multiple_of(base + off, BATCH)
            sp_ref[pl.ds(dst, BATCH), :half] = s_re.astype(BF16)
            sp_ref[pl.ds(dst, BATCH), half:] = s_im.astype(BF16)
            v = v_ref[pl.ds(off, BATCH), :]
            n_re = a_re * s_re - a_im * s_im + v[:, :half]
            n_im = a_re * s_im + a_im * s_re + v[:, half:]
            return n_re, n_im

        s_re, s_im = lax.fori_loop(0, kc, step, (s_ref[d, :, :half], s_ref[d, :, half:]))
        s_ref[d, :, :half] = s_re
        s_ref[d, :, half:] = s_im

        @pl.when(rt == n_tiles - 1)
        def _():
            fin_ref[d, 0] = s_ref[d]

    @pl.when(ph == 0)
    def _():
        recurrence(1, spb_ref, pl.multiple_of((n_tiles - 1 - rt) * rows, rows))

    @pl.when(ph == 1)
    def _():
        recurrence(0, spf_ref, 0)
        s_bwd = spb_ref[pl.ds(pl.multiple_of(rt * rows, rows), rows), :]
        lhs = jnp.concatenate([ucat, spf_ref[...], s_bwd], axis=1)
        y = jnp.dot(lhs, mz_ref[0], preferred_element_type=F32)
        for t in range(S5_CHUNK):
            y_ref[:, BATCH * t:BATCH * (t + 1), :] = (
                y[:, LANES * t:LANES * (t + 1)].reshape(kc, BATCH, LANES).astype(BF16))


def _s5_call(u, mats, init):
    w, mz, a = mats
    n = u.shape[0]
    chunk_rows = S5_CHUNK * BATCH
    n_chunks = n // chunk_rows
    kc = min(32, n_chunks)
    n_tiles = n_chunks // kc
    nb = S5_LANE_BLOCKS

    def tile(ph, rt):
        return rt + (1 - ph) * (n_tiles - 1 - 2 * rt)

    both_dirs = lambda shape: pl.BlockSpec((2, 1) + shape, lambda j, ph, rt: (0, j, 0, 0))
    y, fin = pl.pallas_call(
        functools.partial(_s5_kernel, kc=kc, n_tiles=n_tiles),
        grid=(nb, 2, n_tiles),
        in_specs=[pl.BlockSpec((kc, chunk_rows, LANES), lambda j, ph, rt: (tile(ph, rt), 0, j)),
                  pl.BlockSpec((1, 1, S5_TK, S5_STATE_COLS), lambda j, ph, rt: (1 - ph, j, 0, 0)),
                  pl.BlockSpec((1, S5_TK + 2 * S5_STATE_COLS, S5_TK), lambda j, ph, rt: (j, 0, 0)),
                  both_dirs((2, S5_STATE_COLS // 2)), both_dirs((BATCH, S5_STATE_COLS))],
        out_specs=[pl.BlockSpec((kc, chunk_rows, LANES), lambda j, ph, rt: (ph * rt, 0, j)),
                   both_dirs((BATCH, S5_STATE_COLS))],
        out_shape=[jax.ShapeDtypeStruct((n_chunks, chunk_rows, D_S5), BF16),
                   jax.ShapeDtypeStruct((2, nb, BATCH, S5_STATE_COLS), F32)],
        scratch_shapes=[pltpu.VMEM((2, BATCH, S5_STATE_COLS), F32),
                        pltpu.VMEM((kc * BATCH, S5_STATE_COLS), F32),
                        pltpu.VMEM((kc * BATCH, S5_STATE_COLS), BF16),
                        pltpu.VMEM((n_chunks * BATCH, S5_STATE_COLS), BF16)],
        compiler_params=_cparams(("arbitrary", "arbitrary", "arbitrary")),
        name="s5",
    )(u.reshape(n_chunks, chunk_rows, D_S5), w, mz, a, init)
    return y.reshape(n, D_S5), fin


LRU_HALO_PREV = 2 * BATCH
LRU_HALO_NEXT = BATCH


def _lru_kernel(xp_ref, xc_ref, xn_ref, cw_ref, cb_ref, wg_ref, bg_ref, lam_ref, init_ref,
                h_ref, fin_ref, xpad_ref, a_ref, b_ref, hst_ref, *, tm, n_tiles):
    d = pl.program_id(0)
    i = pl.program_id(1)
    te = i + d * (n_tiles - 1 - 2 * i)

    @pl.when(i == 0)
    def _():
        hst_ref[...] = init_ref[0]

    xpad_ref[0:LRU_HALO_PREV, :] = jnp.where(te > 0, xp_ref[...].astype(F32), 0.0)
    xpad_ref[LRU_HALO_PREV:LRU_HALO_PREV + tm, :] = xc_ref[...].astype(F32)
    xpad_ref[LRU_HALO_PREV + tm:, :] = jnp.where(te < n_tiles - 1, xn_ref[...].astype(F32), 0.0)

    xc = cb_ref[...]
    for k in range(4):
        xc = xc + cw_ref[k:k + 1, :] * xpad_ref[BATCH * k:BATCH * k + tm, :]

    gates = jnp.tanh(jnp.dot(xc.astype(BF16), wg_ref[0], preferred_element_type=F32) + bg_ref[0])
    z = -lam_ref[0]
    rate = (-0.5 * LRU_C) * (jnp.maximum(z, 0.0) + jnp.log1p(jnp.exp(-jnp.abs(z))))
    log_a = rate * gates[:, :D_LRU] + rate
    a = jnp.exp(log_a)
    half_x = 0.5 * xc
    gated_x = half_x * gates[:, D_LRU:] + half_x
    a_ref[...] = a
    b_ref[...] = jnp.sqrt(-jnp.tanh(log_a) * (a * a + 1.0)) * gated_x

    steps = tm // BATCH

    def step(j, h):
        k = j + d * (steps - 1 - 2 * j)
        off = pl.multiple_of(k * BATCH, BATCH)
        h = a_ref[pl.ds(off, BATCH), :] * h + b_ref[pl.ds(off, BATCH), :]
        b_ref[pl.ds(off, BATCH), :] = h
        return h

    h = lax.fori_loop(0, steps, step, hst_ref[...])
    hst_ref[...] = h
    h_ref[0] = b_ref[...].astype(BF16)

    @pl.when(i == n_tiles - 1)
    def _():
        fin_ref[0] = h


def _lru_call(xl, conv_w, conv_b, wg, bg, lam, init):
    n = xl.shape[0]
    tm = 2048
    n_tiles = n // tm

    def tile(d, i):
        return i + d * (n_tiles - 1 - 2 * i)

    prev_blocks = tm // LRU_HALO_PREV
    next_blocks = tm // LRU_HALO_NEXT
    full = lambda shape: pl.BlockSpec(shape, lambda d, i: (0,) * len(shape))
    per_dir = lambda shape: pl.BlockSpec((1,) + shape, lambda d, i: (d,) + (0,) * len(shape))
    return pl.pallas_call(
        functools.partial(_lru_kernel, tm=tm, n_tiles=n_tiles),
        grid=(2, n_tiles),
        in_specs=[pl.BlockSpec((LRU_HALO_PREV, D_LRU),
                               lambda d, i: (jnp.maximum(tile(d, i) * prev_blocks - 1, 0), 0)),
                  pl.BlockSpec((tm, D_LRU), lambda d, i: (tile(d, i), 0)),
                  pl.BlockSpec((LRU_HALO_NEXT, D_LRU),
                               lambda d, i: (jnp.minimum((tile(d, i) + 1) * next_blocks, n // LRU_HALO_NEXT - 1), 0)),
                  full((4, D_LRU)), full((1, D_LRU)),
                  per_dir((D_LRU, 2 * D_LRU)), per_dir((1, 2 * D_LRU)), per_dir((1, D_LRU)),
                  per_dir((BATCH, D_LRU))],
        out_specs=[pl.BlockSpec((1, tm, D_LRU), lambda d, i: (d, tile(d, i), 0)),
                   per_dir((BATCH, D_LRU))],
        out_shape=[jax.ShapeDtypeStruct((2, n, D_LRU), BF16),
                   jax.ShapeDtypeStruct((2, BATCH, D_LRU), F32)],
        scratch_shapes=[pltpu.VMEM((LRU_HALO_PREV + tm + LRU_HALO_NEXT, D_LRU), F32),
                        pltpu.VMEM((tm, D_LRU), F32),
                        pltpu.VMEM((tm, D_LRU), F32),
                        pltpu.VMEM((BATCH, D_LRU), F32)],
        compiler_params=_cparams(("arbitrary", "arbitrary")),
        name="lru",
    )(xl, xl, xl, conv_w, conv_b, wg, bg, lam, init)


def _mix_out_kernel(ys_ref, u_ref, hl_ref, gl_ref, xs_ref, d_ref, wglu_ref, bglu_ref, wout_ref,
                    g1_ref, lng_ref, lnb_ref, sh2_ref, sc2_ref, xo_ref, h2_ref):
    y = ys_ref[...].astype(F32) + d_ref[...] * u_ref[...].astype(F32)
    y = _gelu(y)
    y = y * _sigmoid(jnp.dot(y.astype(BF16), wglu_ref[...], preferred_element_type=F32) + bglu_ref[...])
    y_lru = (hl_ref[0].astype(F32) + hl_ref[1].astype(F32)) * _gelu(gl_ref[...].astype(F32))
    out = jnp.dot(y.astype(BF16), wout_ref[:D_S5, :], preferred_element_type=F32)
    out = out + jnp.dot(y_lru.astype(BF16), wout_ref[D_S5:, :], preferred_element_type=F32)
    g1 = g1_ref[...]
    zsum = DEEPNORM_ALPHA * xs_ref[...] + _per_batch(out, lambda o: o * g1[None])
    xs_new = _ln(zsum) * lng_ref[...] + lnb_ref[...]
    xo_ref[...] = xs_new
    h2_ref[...] = _modulate(xs_new, sh2_ref[...], sc2_ref[...]).astype(BF16)


def _mix_out_call(ys, u, hl, gl, xs, s5_d, w_glu, b_glu, w_out, g1, ln_g, ln_b, sh2, sc2):
    n = xs.shape[0]
    tm = 1024
    full = lambda shape: pl.BlockSpec(shape, lambda i: (0,) * len(shape))
    half = pl.BlockSpec((tm, D_S5), lambda i: (i, 0))
    both = pl.BlockSpec((2, tm, D_S5), lambda i: (0, i, 0))
    row = pl.BlockSpec((tm, D_MODEL), lambda i: (i, 0))
    return pl.pallas_call(
        _mix_out_kernel,
        grid=(n // tm,),
        in_specs=[half, half, both, half, row, full((1, D_S5)), full((D_S5, D_S5)), full((1, D_S5)),
                  full((D_MODEL, D_MODEL)), full((BATCH, D_MODEL)), full((1, D_MODEL)), full((1, D_MODEL)),
                  full((BATCH, D_MODEL)), full((BATCH, D_MODEL))],
        out_specs=[row, row],
        out_shape=[jax.ShapeDtypeStruct((n, D_MODEL), F32), jax.ShapeDtypeStruct((n, D_MODEL), BF16)],
        compiler_params=_cparams(("parallel",)),
        name="mix_out",
    )(ys, u, hl, gl, xs, s5_d, w_glu, b_glu, w_out, g1, ln_g, ln_b, sh2, sc2)


def _ffn_up_kernel(h_ref, w_ref, o_ref):
    o_ref[0] = jnp.dot(h_ref[...], w_ref[0], preferred_element_type=F32).astype(BF16)


def _ffn_up_call(h2, w_up):
    n = h2.shape[0]
    tm = 1024
    return pl.pallas_call(
        _ffn_up_kernel,
        grid=(2, n // tm),
        in_specs=[pl.BlockSpec((tm, D_MODEL), lambda s, i: (i, 0)),
                  pl.BlockSpec((1, D_MODEL, D_FF), lambda s, i: (s, 0, 0))],
        out_specs=pl.BlockSpec((1, tm, D_FF), lambda s, i: (s, i, 0)),
        out_shape=jax.ShapeDtypeStruct((2, n, D_FF), BF16),
        compiler_params=_cparams(("parallel", "parallel")),
        name="ffn_up",
    )(h2, w_up)


GRID_ROW = GRID_W * BATCH
EPILOGUE_ROWS = 256


def _ffn_down_kernel(u_ref, v_ref, cw_ref, cb_ref, wd_ref, xs_ref, g2_ref, lng_ref, lnb_ref,
                     xo_ref, win_ref, act_ref, *, n_tiles, n_chunks, on_grid):
    s = pl.program_id(0)
    c = pl.program_id(1)
    tm = GRID_ROW
    win = win_ref.at[c]

    @pl.when(s == 0)
    def _():
        win[0] = jnp.zeros((tm, FF_CHUNK), BF16)

    @pl.when(s > 0)
    def _():
        edge = jnp.zeros((BATCH, LANES), BF16)
        for k in range(FF_CHUNK // LANES):
            ls = slice(LANES * k, LANES * (k + 1))
            cur = win[1, :, ls]
            nxt = jnp.where(s < n_tiles, u_ref[0, :, ls], jnp.zeros((), BF16))
            w = [cw_ref[t:t + 1, ls].astype(BF16) for t in range(9)]
            if on_grid:
                rows = ((-1, win[0, :, ls], edge, edge), (0, cur, edge, edge), (1, nxt, edge, edge))
            else:
                rows = ((0, cur, win[0, tm - BATCH:, ls], nxt[:BATCH]),)
            conv = cb_ref[:, ls].astype(BF16)
            for dy, mid, before, after in rows:
                left = jnp.concatenate([before, mid[:tm - BATCH]], axis=0)
                right = jnp.concatenate([mid[BATCH:], after], axis=0)
                t0 = 3 * (dy + 1)
                conv = conv + (w[t0] * left + w[t0 + 1] * mid + w[t0 + 2] * right)
            act_ref[:, ls] = _gelu(conv) * v_ref[0, :, ls]
        part = jnp.dot(act_ref[...], wd_ref[...], preferred_element_type=F32)

        @pl.when(c == 0)
        def _():
            xo_ref[...] = part

        @pl.when(c > 0)
        def _():
            xo_ref[...] += part

        @pl.when(c == n_chunks - 1)
        def _():
            g2 = g2_ref[...]
            for r in range(tm // EPILOGUE_ROWS):
                rs = slice(EPILOGUE_ROWS * r, EPILOGUE_ROWS * (r + 1))
                zsum = DEEPNORM_ALPHA * xs_ref[rs, :] + _per_batch(xo_ref[rs, :], lambda o: o * g2[None])
                xo_ref[rs, :] = _ln(zsum) * lng_ref[...] + lnb_ref[...]

        win[0] = win[1]

    win[1] = u_ref[0]


def _ffn_down_call(uv, xs, conv_w, conv_b, w_down, g2, ln_g, ln_b, on_grid):
    n = xs.shape[0]
    tm = GRID_ROW
    n_tiles = n // tm
    n_chunks = D_FF // FF_CHUNK
    full = lambda shape: pl.BlockSpec(shape, lambda s, c: (0,) * len(shape))
    done = lambda s: jnp.maximum(s - 1, 0)
    row = pl.BlockSpec((tm, D_MODEL), lambda s, c: (done(s), 0))
    return pl.pallas_call(
        functools.partial(_ffn_down_kernel, n_tiles=n_tiles, n_chunks=n_chunks, on_grid=on_grid),
        grid=(n_tiles + 1, n_chunks),
        in_specs=[pl.BlockSpec((1, tm, FF_CHUNK), lambda s, c: (0, jnp.minimum(s, n_tiles - 1), c)),
                  pl.BlockSpec((1, tm, FF_CHUNK), lambda s, c: (1, done(s), c)),
                  pl.BlockSpec((9, FF_CHUNK), lambda s, c: (0, c)),
                  pl.BlockSpec((1, FF_CHUNK), lambda s, c: (0, c)),
                  pl.BlockSpec((FF_CHUNK, D_MODEL), lambda s, c: (c, 0)),
                  row, full((BATCH, D_MODEL)), full((1, D_MODEL)), full((1, D_MODEL))],
        out_specs=row,
        out_shape=jax.ShapeDtypeStruct((n, D_MODEL), F32),
        scratch_shapes=[pltpu.VMEM((n_chunks, 2, tm, FF_CHUNK), BF16),
                        pltpu.VMEM((tm, FF_CHUNK), BF16)],
        compiler_params=_cparams(("arbitrary", "arbitrary")),
        name="ffn_down_grid" if on_grid else "ffn_down_seq",
    )(uv, uv, conv_w, conv_b, w_down, xs, g2, ln_g, ln_b)


def _time_major(x):
    b, length, d = x.shape
    return x.transpose(1, 0, 2).reshape(length * b, d)


def kernel(x, c, ctx, c_ctx, w_mod, b_mod, w_in, s5_lam_re, s5_lam_im, s5_log_dt, s5_b_re, s5_b_im, s5_c_re, s5_c_im, s5_d, s5_w_glu, s5_b_glu, lru_conv_w, lru_conv_b, lru_w_a, lru_b_a, lru_w_x, lru_b_x, lru_lam, w_out, ln1_g, ln1_b, ffn_w_up, ffn_conv_w, ffn_conv_b, ffn_w_down, ln2_g, ln2_b):
    bsz, seq, _ = x.shape
    xs = _time_major(x)
    cs = _time_major(ctx)

    c_all = jnp.zeros((MOD_ROWS, D_MODEL), F32).at[:bsz].set(c).at[bsz].set(c_ctx)
    mods = _mod_call(c_all, w_mod, b_mod)

    eye_h = jnp.eye(LRU_HEADS, dtype=F32)
    s5_zero = jnp.zeros((2, S5_LANE_BLOCKS, BATCH, S5_STATE_COLS), F32)
    lru_zero = jnp.zeros((2, BATCH, D_LRU), F32)

    for l in range(DEPTH):
        last = l == DEPTH - 1
        lat = [mods[l, :bsz, k * D_MODEL:(k + 1) * D_MODEL] for k in range(N_MOD)]
        con = [jnp.broadcast_to(mods[l, bsz:bsz + 1, k * D_MODEL:(k + 1) * D_MODEL], (BATCH, D_MODEL))
               for k in range(N_MOD)]

        w_in_l = w_in[l].astype(BF16)
        mats = _s5_matrices(s5_lam_re[l], s5_lam_im[l], s5_log_dt[l], s5_b_re[l], s5_b_im[l],
                            s5_c_re[l], s5_c_im[l])
        dense = lambda wh: jnp.einsum("dhij,hk->dhikj", wh, eye_h).reshape(2, D_LRU, D_LRU)
        wg = (0.5 * jnp.concatenate([dense(lru_w_a[l]), dense(lru_w_x[l])], axis=-1)).astype(BF16)
        bg = 0.5 * jnp.concatenate([lru_b_a[l], lru_b_x[l]], axis=-1).reshape(2, 1, 2 * D_LRU)
        lam = lru_lam[l].reshape(2, 1, D_LRU)
        conv_b = lru_conv_b[l].reshape(1, D_LRU)
        d_skip = s5_d[l].reshape(1, D_S5)
        w_glu = s5_w_glu[l].astype(BF16)
        b_glu = s5_b_glu[l].reshape(1, D_S5)
        w_out_l = w_out[l].astype(BF16)
        w_up = ffn_w_up[l].reshape(D_MODEL, 2, D_FF).transpose(1, 0, 2).astype(BF16)
        f_conv_w = ffn_conv_w[l].reshape(9, D_FF)
        f_conv_b = ffn_conv_b[l].reshape(1, D_FF)
        w_down = ffn_w_down[l].astype(BF16)
        ln1 = (ln1_g[l].reshape(1, D_MODEL), ln1_b[l].reshape(1, D_MODEL))
        ln2 = (ln2_g[l].reshape(1, D_MODEL), ln2_b[l].reshape(1, D_MODEL))

        def mixers(stream, mod, s5_init, lru_init):
            u, xl, gl = _mix_in_call(stream, mod[0], mod[1], w_in_l)
            ys, s5_fin = _s5_call(u, mats, s5_init)
            hl, lru_fin = _lru_call(xl, lru_conv_w[l], conv_b, wg, bg, lam, lru_init)
            return (ys, u, hl, gl), (s5_fin, lru_fin)

        def finish(stream, mixed, mod, on_grid):
            xs_mid, h2 = _mix_out_call(*mixed, stream, d_skip, w_glu, b_glu, w_out_l, mod[2], *ln1, mod[3], mod[4])
            uv = _ffn_up_call(h2, w_up)
            return _ffn_down_call(uv, xs_mid, f_conv_w, f_conv_b, w_down, mod[5], *ln2, on_grid)

        mixed_c, (s5_fin, lru_fin) = mixers(cs, con, s5_zero, lru_zero)
        mixed_x, _ = mixers(xs, lat, s5_fin, lru_fin)
        xs = finish(xs, mixed_x, lat, True)
        if not last:
            cs = finish(cs, mixed_c, con, False)

    return xs.reshape(seq, bsz, D_MODEL).transpose(1, 0, 2)
```

```python
import functools

import jax
import jax.numpy as jnp
from jax import lax
from jax.experimental import pallas as pl
from jax.experimental.pallas import tpu as pltpu

F32 = jnp.float32
BF16 = jnp.bfloat16

D_MODEL = 1024
BATCH = 16
DEPTH = 4
GRID_W = 64
D_S5 = 512
D_LRU = 512
S5_GROUP_CH = 16
S5_GROUPS = 32
S5_STATE = 64
LRU_HEADS = 8
LRU_HEAD_DIM = 64
LRU_C = 8.0
D_FF = 2816
N_MOD = 6
LN_EPS = 1e-6
DEEPNORM_ALPHA = (2 * DEPTH) ** 0.25

LANES = 128
GROUPS_PER_LANE_BLOCK = LANES // S5_GROUP_CH
S5_LANE_BLOCKS = D_S5 // LANES
S5_STATE_COLS = 2 * GROUPS_PER_LANE_BLOCK * S5_STATE
S5_CHUNK = 8
S5_TK = S5_CHUNK * LANES
MOD_ROWS = 2 * BATCH
FF_CHUNK = 1408
VMEM_LIMIT = 56 * 1024 * 1024
SHIFT1, SCALE1, GATE1, SHIFT2, SCALE2, GATE2 = range(N_MOD)


def _cparams(sem):
    return pltpu.CompilerParams(dimension_semantics=sem, vmem_limit_bytes=VMEM_LIMIT)


def _layer_spec(shape, l):
    return pl.BlockSpec((1,) + shape, lambda *_: (l,) + (0,) * len(shape))


def _mod_spec(l, who, slot):
    return pl.BlockSpec((1, BATCH, D_MODEL), lambda *_: (l, who, slot))


def _sigmoid(x):
    return 0.5 * jnp.tanh(0.5 * x) + 0.5


def _gelu(x):
    return 0.5 * x * (1.0 + jnp.tanh(0.7978845608028654 * (x + 0.044715 * (x * x * x))))


def _ln(x):
    mu = jnp.mean(x, axis=-1, keepdims=True)
    xc = x - mu
    var = jnp.mean(xc * xc, axis=-1, keepdims=True)
    return xc * lax.rsqrt(var + LN_EPS)


def _per_batch(x, fn):
    rows, d = x.shape
    return fn(x.reshape(rows // BATCH, BATCH, d)).reshape(rows, d)


def _modulate(x, shift, scale):
    return _per_batch(_ln(x), lambda h: h * (1.0 + scale)[None] + shift[None])


def _mod_kernel(c_ref, w_ref, b_ref, o_ref):
    c = c_ref[...]
    s = c * _sigmoid(c)
    o_ref[0] = jnp.dot(s, w_ref[0], preferred_element_type=F32) + b_ref[0]


def _mod_call(c_all, w_mod, b_mod):
    tn = 1536
    return pl.pallas_call(
        _mod_kernel,
        grid=(DEPTH, N_MOD * D_MODEL // tn),
        in_specs=[pl.BlockSpec((MOD_ROWS, D_MODEL), lambda l, j: (0, 0)),
                  pl.BlockSpec((1, D_MODEL, tn), lambda l, j: (l, 0, j)),
                  pl.BlockSpec((1, 1, tn), lambda l, j: (l, 0, j))],
        out_specs=pl.BlockSpec((1, MOD_ROWS, tn), lambda l, j: (l, 0, j)),
        out_shape=jax.ShapeDtypeStruct((DEPTH, MOD_ROWS, N_MOD * D_MODEL), F32),
        compiler_params=_cparams(("parallel", "parallel")),
        name="mod",
    )(c_all, w_mod, b_mod.reshape(DEPTH, 1, N_MOD * D_MODEL))


def _mix_in_kernel(x_ref, sh_ref, sc_ref, w_ref, u_ref, xl_ref, gl_ref):
    h = _modulate(x_ref[...], sh_ref[0], sc_ref[0]).astype(BF16)
    p = jnp.dot(h, w_ref[0], preferred_element_type=F32)
    u_ref[...] = p[:, :D_S5].astype(BF16)
    xl_ref[...] = p[:, D_S5:D_S5 + D_LRU].astype(BF16)
    gl_ref[...] = p[:, D_S5 + D_LRU:].astype(BF16)


def _mix_in_call(xs, mods, w_in, l, who):
    n = xs.shape[0]
    tm = 1024
    out = pl.BlockSpec((tm, D_S5), lambda i: (i, 0))
    return pl.pallas_call(
        _mix_in_kernel,
        grid=(n // tm,),
        in_specs=[pl.BlockSpec((tm, D_MODEL), lambda i: (i, 0)),
                  _mod_spec(l, who, SHIFT1), _mod_spec(l, who, SCALE1),
                  _layer_spec((D_MODEL, D_S5 + 2 * D_LRU), l)],
        out_specs=[out, out, out],
        out_shape=[jax.ShapeDtypeStruct((n, D_S5), BF16)] * 3,
        compiler_params=_cparams(("parallel",)),
        name="mix_in",
    )(xs, mods, mods, w_in)


def _s5_matrices(lam_re, lam_im, log_dt, b_re, b_im, c_re, c_im):
    T = S5_CHUNK
    lam = lax.complex(lam_re, lam_im)
    lam_dt = lam * jnp.exp(log_dt)[..., None]
    lam_bar = jnp.exp(lam_dt)
    b_bar = ((lam_bar - 1.0) / lam)[..., None] * lax.complex(b_re, b_im)
    c_mat = lax.complex(c_re, c_im)
    steps = jnp.arange(T + 1, dtype=F32)
    pw = jnp.exp(steps[:, None, None, None] * lam_dt[None])
    kd = jnp.real(jnp.einsum("dgop,ndgp,dgpi->ndgio", c_mat, pw[:T], b_bar))
    nb, gpb = S5_LANE_BLOCKS, GROUPS_PER_LANE_BLOCK
    half = gpb * S5_STATE

    lane_group = jnp.arange(LANES) // S5_GROUP_CH
    state_group = jnp.arange(half) // S5_STATE
    rep_ch = jnp.tile(jnp.eye(S5_GROUP_CH, dtype=BF16), (1, gpb))
    rep_st = jnp.tile(jnp.eye(S5_STATE, dtype=BF16), (1, gpb))
    same_ll = (lane_group[:, None] == lane_group[None, :]).astype(BF16)
    same_ls = (lane_group[:, None] == state_group[None, :]).astype(BF16)
    same_sl = same_ls.T

    def spread(x, rep, same, out):
        return jnp.einsum(out, x.astype(BF16), rep, preferred_element_type=BF16) * same

    ar = jnp.arange(T)
    kd = kd.at[0, 0].add(kd[0, 1])
    kb = spread(kd.reshape(T, 2, nb, LANES, S5_GROUP_CH), rep_ch, same_ll, "ndjxo,oq->dnjxq")
    m = jnp.stack([jnp.concatenate([kb[0, b - a] if b >= a else kb[1, a - b] for b in range(T)], axis=-1)
                   for a in range(T)], axis=1).reshape(nb, S5_TK, S5_TK)

    e_in = jnp.stack([T - 1 - ar, ar])
    pw_in = jnp.stack([pw[e_in[d], d] for d in range(2)])
    wc = (pw_in[:, :, :, None, :] * jnp.swapaxes(b_bar, -1, -2)[:, None]).reshape(2, T, nb, LANES, S5_STATE)
    w = jnp.concatenate([spread(part, rep_st, same_ls, "dtjxp,pq->djtxq")
                         for part in (jnp.real(wc), jnp.imag(wc))], axis=-1)
    w = w.reshape(2, nb, S5_TK, S5_STATE_COLS)

    f_out = jnp.stack([ar + 1, T - ar])
    pw_out = jnp.stack([pw[f_out[d], d] for d in range(2)])
    zc = (jnp.swapaxes(c_mat, -1, -2)[:, None] * pw_out[..., None]).reshape(2, T, nb, half, S5_GROUP_CH)
    z = jnp.concatenate([
        jnp.concatenate([spread(part[:, t], rep_ch, same_sl, "djyo,oq->djyq") for t in range(T)], axis=-1)
        for part in (jnp.real(zc), -jnp.imag(zc))], axis=-2)

    a_t = pw[T].reshape(2, nb, half)
    a = jnp.stack([jnp.real(a_t), jnp.imag(a_t)], axis=2)
    return w, jnp.concatenate([m, z[0], z[1]], axis=1), a


def _s5_kernel(u_ref, w_ref, mz_ref, a_ref, init_ref, y_ref, fin_ref, s_ref, v_ref, spf_ref, spb_ref, *,
               kc, n_tiles):
    ph = pl.program_id(1)
    rt = pl.program_id(2)
    half = S5_STATE_COLS // 2
    rows = kc * BATCH

    @pl.when((ph == 0) & (rt == 0))
    def _():
        s_ref[...] = init_ref[:, 0]

    ucat = jnp.concatenate(
        [u_ref[:, BATCH * t:BATCH * (t + 1), :].reshape(rows, LANES) for t in range(S5_CHUNK)], axis=1)

    def recurrence(d, sp_ref, base):
        v_ref[...] = jnp.dot(ucat, w_ref[0, 0, 0], preferred_element_type=F32)
        a_re = jnp.broadcast_to(a_ref[0, d, 0, 0:1, :], (BATCH, half))
        a_im = jnp.broadcast_to(a_ref[0, d, 0, 1:2, :], (BATCH, half))

        def step(i, carry):
            s_re, s_im = carry
            off = pl.multiple_of((kc - 1 - i if d else i) * BATCH, BATCH)
            dst = pl.multiple_of(base + off, BATCH)
            sp_ref[pl.ds(dst, BATCH), :half] = s_re.astype(BF16)
            sp_ref[pl.ds(dst, BATCH), half:] = s_im.astype(BF16)
            v = v_ref[pl.ds(off, BATCH), :]
            n_re = a_re * s_re - a_im * s_im + v[:, :half]
            n_im = a_re * s_im + a_im * s_re + v[:, half:]
            return n_re, n_im

        s_re, s_im = lax.fori_loop(0, kc, step, (s_ref[d, :, :half], s_ref[d, :, half:]))
        s_ref[d, :, :half] = s_re
        s_ref[d, :, half:] = s_im

        @pl.when(rt == n_tiles - 1)
        def _():
            fin_ref[d, 0] = s_ref[d]

    @pl.when(ph == 0)
    def _():
        recurrence(1, spb_ref, pl.multiple_of((n_tiles - 1 - rt) * rows, rows))

    @pl.when(ph == 1)
    def _():
        recurrence(0, spf_ref, 0)
        s_bwd = spb_ref[pl.ds(pl.multiple_of(rt * rows, rows), rows), :]
        lhs = jnp.concatenate([ucat, spf_ref[...], s_bwd], axis=1)
        y = jnp.dot(lhs, mz_ref[0, 0], preferred_element_type=F32)
        for t in range(S5_CHUNK):
            y_ref[:, BATCH * t:BATCH * (t + 1), :] = (
                y[:, LANES * t:LANES * (t + 1)].reshape(kc, BATCH, LANES).astype(BF16))


def _s5_call(u, mats, init, l):
    w, mz, a = mats
    n = u.shape[0]
    chunk_rows = S5_CHUNK * BATCH
    n_chunks = n // chunk_rows
    kc = min(32, n_chunks)
    n_tiles = n_chunks // kc
    nb = S5_LANE_BLOCKS

    def tile(ph, rt):
        return rt + (1 - ph) * (n_tiles - 1 - 2 * rt)

    both_dirs = lambda shape: pl.BlockSpec((2, 1) + shape, lambda j, ph, rt: (0, j, 0, 0))
    y, fin = pl.pallas_call(
        functools.partial(_s5_kernel, kc=kc, n_tiles=n_tiles),
        grid=(nb, 2, n_tiles),
        in_specs=[pl.BlockSpec((kc, chunk_rows, LANES), lambda j, ph, rt: (tile(ph, rt), 0, j)),
                  pl.BlockSpec((1, 1, 1, S5_TK, S5_STATE_COLS), lambda j, ph, rt: (l, 1 - ph, j, 0, 0)),
                  pl.BlockSpec((1, 1, S5_TK + 2 * S5_STATE_COLS, S5_TK), lambda j, ph, rt: (l, j, 0, 0)),
                  pl.BlockSpec((1, 2, 1, 2, S5_STATE_COLS // 2), lambda j, ph, rt: (l, 0, j, 0, 0)),
                  both_dirs((BATCH, S5_STATE_COLS))],
        out_specs=[pl.BlockSpec((kc, chunk_rows, LANES), lambda j, ph, rt: (ph * rt, 0, j)),
                   both_dirs((BATCH, S5_STATE_COLS))],
        out_shape=[jax.ShapeDtypeStruct((n_chunks, chunk_rows, D_S5), BF16),
                   jax.ShapeDtypeStruct((2, nb, BATCH, S5_STATE_COLS), F32)],
        scratch_shapes=[pltpu.VMEM((2, BATCH, S5_STATE_COLS), F32),
                        pltpu.VMEM((kc * BATCH, S5_STATE_COLS), F32),
                        pltpu.VMEM((kc * BATCH, S5_STATE_COLS), BF16),
                        pltpu.VMEM((n_chunks * BATCH, S5_STATE_COLS), BF16)],
        compiler_params=_cparams(("arbitrary", "arbitrary", "arbitrary")),
        name="s5",
    )(u.reshape(n_chunks, chunk_rows, D_S5), w, mz, a, init)
    return y.reshape(n, D_S5), fin


LRU_HALO_PREV = 2 * BATCH
LRU_HALO_NEXT = BATCH


def _lru_kernel(xp_ref, xc_ref, xn_ref, cw_ref, cb_ref, wg_ref, bg_ref, lam_ref, init_ref,
                h_ref, fin_ref, xpad_ref, a_ref, b_ref, hst_ref, *, tm, n_tiles):
    d = pl.program_id(0)
    i = pl.program_id(1)
    te = i + d * (n_tiles - 1 - 2 * i)

    @pl.when(i == 0)
    def _():
        hst_ref[...] = init_ref[0]

    xpad_ref[0:LRU_HALO_PREV, :] = jnp.where(te > 0, xp_ref[...].astype(F32), 0.0)
    xpad_ref[LRU_HALO_PREV:LRU_HALO_PREV + tm, :] = xc_ref[...].astype(F32)
    xpad_ref[LRU_HALO_PREV + tm:, :] = jnp.where(te < n_tiles - 1, xn_ref[...].astype(F32), 0.0)

    xc = cb_ref[0]
    for k in range(4):
        xc = xc + cw_ref[0, k:k + 1, :] * xpad_ref[BATCH * k:BATCH * k + tm, :]

    gates = jnp.tanh(jnp.dot(xc.astype(BF16), wg_ref[0, 0], preferred_element_type=F32) + bg_ref[0, 0])
    z = -lam_ref[0, 0]
    rate = (-0.5 * LRU_C) * (jnp.maximum(z, 0.0) + jnp.log1p(jnp.exp(-jnp.abs(z))))
    log_a = rate * gates[:, :D_LRU] + rate
    a = jnp.exp(log_a)
    half_x = 0.5 * xc
    gated_x = half_x * gates[:, D_LRU:] + half_x
    a_ref[...] = a
    b_ref[...] = jnp.sqrt(-jnp.tanh(log_a) * (a * a + 1.0)) * gated_x

    steps = tm // BATCH

    def step(j, h):
        k = j + d * (steps - 1 - 2 * j)
        off = pl.multiple_of(k * BATCH, BATCH)
        h = a_ref[pl.ds(off, BATCH), :] * h + b_ref[pl.ds(off, BATCH), :]
        b_ref[pl.ds(off, BATCH), :] = h
        return h

    h = lax.fori_loop(0, steps, step, hst_ref[...])
    hst_ref[...] = h
    h_ref[0] = b_ref[...].astype(BF16)

    @pl.when(i == n_tiles - 1)
    def _():
        fin_ref[0] = h


def _lru_call(xl, conv_w, conv_b, wg, bg, lam, init, l):
    n = xl.shape[0]
    tm = 2048
    n_tiles = n // tm

    def tile(d, i):
        return i + d * (n_tiles - 1 - 2 * i)

    prev_blocks = tm // LRU_HALO_PREV
    next_blocks = tm // LRU_HALO_NEXT
    layer_dir = lambda shape: pl.BlockSpec((1, 1) + shape, lambda d, i: (l, d) + (0,) * len(shape))
    per_dir = lambda shape: pl.BlockSpec((1,) + shape, lambda d, i: (d,) + (0,) * len(shape))
    return pl.pallas_call(
        functools.partial(_lru_kernel, tm=tm, n_tiles=n_tiles),
        grid=(2, n_tiles),
        in_specs=[pl.BlockSpec((LRU_HALO_PREV, D_LRU),
                               lambda d, i: (jnp.maximum(tile(d, i) * prev_blocks - 1, 0), 0)),
                  pl.BlockSpec((tm, D_LRU), lambda d, i: (tile(d, i), 0)),
                  pl.BlockSpec((LRU_HALO_NEXT, D_LRU),
                               lambda d, i: (jnp.minimum((tile(d, i) + 1) * next_blocks, n // LRU_HALO_NEXT - 1), 0)),
                  _layer_spec((4, D_LRU), l), _layer_spec((1, D_LRU), l),
                  layer_dir((D_LRU, 2 * D_LRU)), layer_dir((1, 2 * D_LRU)), layer_dir((1, D_LRU)),
                  per_dir((BATCH, D_LRU))],
        out_specs=[pl.BlockSpec((1, tm, D_LRU), lambda d, i: (d, tile(d, i), 0)),
                   per_dir((BATCH, D_LRU))],
        out_shape=[jax.ShapeDtypeStruct((2, n, D_LRU), BF16),
                   jax.ShapeDtypeStruct((2, BATCH, D_LRU), F32)],
        scratch_shapes=[pltpu.VMEM((LRU_HALO_PREV + tm + LRU_HALO_NEXT, D_LRU), F32),
                        pltpu.VMEM((tm, D_LRU), F32),
                        pltpu.VMEM((tm, D_LRU), F32),
                        pltpu.VMEM((BATCH, D_LRU), F32)],
        compiler_params=_cparams(("arbitrary", "arbitrary")),
        name="lru",
    )(xl, xl, xl, conv_w, conv_b, wg, bg, lam, init)


def _mix_out_kernel(ys_ref, u_ref, hl_ref, gl_ref, xs_ref, d_ref, wglu_ref, bglu_ref, wout_ref,
                    g1_ref, lng_ref, lnb_ref, sh2_ref, sc2_ref, xo_ref, h2_ref):
    y = (ys_ref[...].astype(F32) + d_ref[0] * u_ref[...].astype(F32)).astype(BF16)
    y = _gelu(y)
    glu = jnp.dot(y, wglu_ref[0], preferred_element_type=F32) + bglu_ref[0]
    y = y * _sigmoid(glu.astype(BF16))
    y_lru = (hl_ref[0] + hl_ref[1]) * _gelu(gl_ref[...])
    out = jnp.dot(y, wout_ref[0, :D_S5, :], preferred_element_type=F32)
    out = out + jnp.dot(y_lru, wout_ref[0, D_S5:, :], preferred_element_type=F32)
    g1 = g1_ref[0]
    zsum = DEEPNORM_ALPHA * xs_ref[...] + _per_batch(out, lambda o: o * g1[None])
    xs_new = _ln(zsum) * lng_ref[0] + lnb_ref[0]
    xo_ref[...] = xs_new
    h2_ref[...] = _modulate(xs_new, sh2_ref[0], sc2_ref[0]).astype(BF16)


def _mix_out_call(ys, u, hl, gl, xs, mods, s5_d, w_glu, b_glu, w_out, ln_g, ln_b, l, who):
    n = xs.shape[0]
    tm = 1024
    half = pl.BlockSpec((tm, D_S5), lambda i: (i, 0))
    both = pl.BlockSpec((2, tm, D_S5), lambda i: (0, i, 0))
    row = pl.BlockSpec((tm, D_MODEL), lambda i: (i, 0))
    return pl.pallas_call(
        _mix_out_kernel,
        grid=(n // tm,),
        in_specs=[half, half, both, half, row,
                  _layer_spec((1, D_S5), l), _layer_spec((D_S5, D_S5), l), _layer_spec((1, D_S5), l),
                  _layer_spec((D_MODEL, D_MODEL), l), _mod_spec(l, who, GATE1),
                  _layer_spec((1, D_MODEL), l), _layer_spec((1, D_MODEL), l),
                  _mod_spec(l, who, SHIFT2), _mod_spec(l, who, SCALE2)],
        out_specs=[row, row],
        out_shape=[jax.ShapeDtypeStruct((n, D_MODEL), F32), jax.ShapeDtypeStruct((n, D_MODEL), BF16)],
        compiler_params=_cparams(("parallel",)),
        name="mix_out",
    )(ys, u, hl, gl, xs, s5_d, w_glu, b_glu, w_out, mods, ln_g, ln_b, mods, mods)


def _ffn_up_kernel(h_ref, w_ref, o_ref):
    o_ref[0] = jnp.dot(h_ref[...], w_ref[0, 0], preferred_element_type=F32).astype(BF16)


def _ffn_up_call(h2, w_up, l):
    n = h2.shape[0]
    tm = 1024
    return pl.pallas_call(
        _ffn_up_kernel,
        grid=(2, n // tm),
        in_specs=[pl.BlockSpec((tm, D_MODEL), lambda s, i: (i, 0)),
                  pl.BlockSpec((1, 1, D_MODEL, D_FF), lambda s, i: (l, s, 0, 0))],
        out_specs=pl.BlockSpec((1, tm, D_FF), lambda s, i: (s, i, 0)),
        out_shape=jax.ShapeDtypeStruct((2, n, D_FF), BF16),
        compiler_params=_cparams(("parallel", "parallel")),
        name="ffn_up",
    )(h2, w_up)


GRID_ROW = GRID_W * BATCH
EPILOGUE_ROWS = 256


def _ffn_down_kernel(u_ref, v_ref, cw_ref, cb_ref, wd_ref, xs_ref, g2_ref, lng_ref, lnb_ref,
                     xo_ref, win_ref, act_ref, *, n_tiles, n_chunks, on_grid):
    s = pl.program_id(0)
    c = pl.program_id(1)
    tm = GRID_ROW
    win = win_ref.at[c]

    @pl.when(s == 0)
    def _():
        win[0] = jnp.zeros((tm, FF_CHUNK), BF16)

    @pl.when(s > 0)
    def _():
        edge = jnp.zeros((BATCH, LANES), BF16)
        for k in range(FF_CHUNK // LANES):
            ls = slice(LANES * k, LANES * (k + 1))
            cur = win[1, :, ls]
            nxt = jnp.where(s < n_tiles, u_ref[0, :, ls], jnp.zeros((), BF16))
            w = [cw_ref[0, t:t + 1, ls].astype(BF16) for t in range(9)]
            if on_grid:
                rows = ((-1, win[0, :, ls], edge, edge), (0, cur, edge, edge), (1, nxt, edge, edge))
            else:
                rows = ((0, cur, win[0, tm - BATCH:, ls], nxt[:BATCH]),)
            conv = cb_ref[0, :, ls].astype(BF16)
            for dy, mid, before, after in rows:
                left = jnp.concatenate([before, mid[:tm - BATCH]], axis=0)
                right = jnp.concatenate([mid[BATCH:], after], axis=0)
                t0 = 3 * (dy + 1)
                conv = conv + (w[t0] * left + w[t0 + 1] * mid + w[t0 + 2] * right)
            act_ref[:, ls] = _gelu(conv) * v_ref[0, :, ls]
        part = jnp.dot(act_ref[...], wd_ref[0], preferred_element_type=F32)

        @pl.when(c == 0)
        def _():
            xo_ref[...] = part

        @pl.when(c > 0)
        def _():
            xo_ref[...] += part

        @pl.when(c == n_chunks - 1)
        def _():
            g2 = g2_ref[0]
            for r in range(tm // EPILOGUE_ROWS):
                rs = slice(EPILOGUE_ROWS * r, EPILOGUE_ROWS * (r + 1))
                zsum = DEEPNORM_ALPHA * xs_ref[rs, :] + _per_batch(xo_ref[rs, :], lambda o: o * g2[None])
                xo_ref[rs, :] = _ln(zsum) * lng_ref[0] + lnb_ref[0]

        win[0] = win[1]

    win[1] = u_ref[0]


def _ffn_down_call(uv, xs, mods, conv_w, conv_b, w_down, ln_g, ln_b, l, who, on_grid):
    n = xs.shape[0]
    tm = GRID_ROW
    n_tiles = n // tm
    n_chunks = D_FF // FF_CHUNK
    done = lambda s: jnp.maximum(s - 1, 0)
    row = pl.BlockSpec((tm, D_MODEL), lambda s, c: (done(s), 0))
    return pl.pallas_call(
        functools.partial(_ffn_down_kernel, n_tiles=n_tiles, n_chunks=n_chunks, on_grid=on_grid),
        grid=(n_tiles + 1, n_chunks),
        in_specs=[pl.BlockSpec((1, tm, FF_CHUNK), lambda s, c: (0, jnp.minimum(s, n_tiles - 1), c)),
                  pl.BlockSpec((1, tm, FF_CHUNK), lambda s, c: (1, done(s), c)),
                  pl.BlockSpec((1, 9, FF_CHUNK), lambda s, c: (l, 0, c)),
                  pl.BlockSpec((1, 1, FF_CHUNK), lambda s, c: (l, 0, c)),
                  pl.BlockSpec((1, FF_CHUNK, D_MODEL), lambda s, c: (l, c, 0)),
                  row, _mod_spec(l, who, GATE2), _layer_spec((1, D_MODEL), l), _layer_spec((1, D_MODEL), l)],
        out_specs=row,
        out_shape=jax.ShapeDtypeStruct((n, D_MODEL), F32),
        scratch_shapes=[pltpu.VMEM((n_chunks, 2, tm, FF_CHUNK), BF16),
                        pltpu.VMEM((tm, FF_CHUNK), BF16)],
        compiler_params=_cparams(("arbitrary", "arbitrary")),
        name="ffn_down_grid" if on_grid else "ffn_down_seq",
    )(uv, uv, conv_w, conv_b, w_down, xs, mods, ln_g, ln_b)


def _time_major(x):
    b, length, d = x.shape
    return x.transpose(1, 0, 2).reshape(length * b, d)


def kernel(x, c, ctx, c_ctx, w_mod, b_mod, w_in, s5_lam_re, s5_lam_im, s5_log_dt, s5_b_re, s5_b_im, s5_c_re, s5_c_im, s5_d, s5_w_glu, s5_b_glu, lru_conv_w, lru_conv_b, lru_w_a, lru_b_a, lru_w_x, lru_b_x, lru_lam, w_out, ln1_g, ln1_b, ffn_w_up, ffn_conv_w, ffn_conv_b, ffn_w_down, ln2_g, ln2_b):
    bsz, seq, _ = x.shape
    xs = _time_major(x)
    cs = _time_major(ctx)

    c_all = jnp.concatenate([c, jnp.broadcast_to(c_ctx, (BATCH, D_MODEL))], axis=0)
    mods = _mod_call(c_all, w_mod, b_mod)

    row = lambda v: v.reshape(DEPTH, 1, v.shape[-1])
    w_in_b = w_in.astype(BF16)
    mats = jax.vmap(_s5_matrices)(s5_lam_re, s5_lam_im, s5_log_dt, s5_b_re, s5_b_im, s5_c_re, s5_c_im)
    eye_h = jnp.eye(LRU_HEADS, dtype=F32)
    dense = lambda wh: jnp.einsum("ldhij,hk->ldhikj", wh, eye_h).reshape(DEPTH, 2, D_LRU, D_LRU)
    wg = (0.5 * jnp.concatenate([dense(lru_w_a), dense(lru_w_x)], axis=-1)).astype(BF16)
    bg = 0.5 * jnp.concatenate([lru_b_a, lru_b_x], axis=-1).reshape(DEPTH, 2, 1, 2 * D_LRU)
    lam = lru_lam.reshape(DEPTH, 2, 1, D_LRU)
    w_glu = s5_w_glu.astype(BF16)
    w_out_b = w_out.astype(BF16)
    w_up = ffn_w_up.reshape(DEPTH, D_MODEL, 2, D_FF).transpose(0, 2, 1, 3).astype(BF16)
    f_conv_w = ffn_conv_w.reshape(DEPTH, 9, D_FF)
    w_down = ffn_w_down.astype(BF16)
    ln1 = (row(ln1_g), row(ln1_b))
    ln2 = (row(ln2_g), row(ln2_b))

    s5_zero = jnp.zeros((2, S5_LANE_BLOCKS, BATCH, S5_STATE_COLS), F32)
    lru_zero = jnp.zeros((2, BATCH, D_LRU), F32)

    for l in range(DEPTH):
        last = l == DEPTH - 1

        def mixers(stream, who, s5_init, lru_init):
            u, xl, gl = _mix_in_call(stream, mods, w_in_b, l, who)
            ys, s5_fin = _s5_call(u, mats, s5_init, l)
            hl, lru_fin = _lru_call(xl, lru_conv_w, row(lru_conv_b), wg, bg, lam, lru_init, l)
            return (ys, u, hl, gl), (s5_fin, lru_fin)

        def finish(stream, mixed, who, on_grid):
            xs_mid, h2 = _mix_out_call(*mixed, stream, mods, row(s5_d), w_glu, row(s5_b_glu), w_out_b, *ln1, l, who)
            uv = _ffn_up_call(h2, w_up, l)
            return _ffn_down_call(uv, xs_mid, mods, f_conv_w, row(ffn_conv_b), w_down, *ln2, l, who, on_grid)

        mixed_c, (s5_fin, lru_fin) = mixers(cs, 1, s5_zero, lru_zero)
        mixed_x, _ = mixers(xs, 0, s5_fin, lru_fin)
        xs = finish(xs, mixed_x, 0, True)
        if not last:
            cs = finish(cs, mixed_c, 1, False)

    return xs.reshape(seq, bsz, D_MODEL).transpose(1, 0, 2)
```

```python
import functools

import jax
import jax.numpy as jnp
from jax import lax
from jax.experimental import pallas as pl
from jax.experimental.pallas import tpu as pltpu

F32 = jnp.float32
BF16 = jnp.bfloat16

D_MODEL = 1024
BATCH = 16
DEPTH = 4
GRID_W = 64
D_S5 = 512
D_LRU = 512
S5_GROUP_CH = 16
S5_GROUPS = 32
S5_STATE = 64
LRU_HEADS = 8
LRU_HEAD_DIM = 64
LRU_C = 8.0
D_FF = 2816
N_MOD = 6
LN_EPS = 1e-6
DEEPNORM_ALPHA = (2 * DEPTH) ** 0.25

LANES = 128
GROUPS_PER_LANE_BLOCK = LANES // S5_GROUP_CH
S5_LANE_BLOCKS = D_S5 // LANES
S5_STATE_COLS = 2 * GROUPS_PER_LANE_BLOCK * S5_STATE
S5_CHUNK = 8
S5_TK = S5_CHUNK * LANES
MOD_ROWS = 2 * BATCH
ROW_BLOCK = 256
FF_CHUNK = 1408
VMEM_LIMIT = 56 * 1024 * 1024
SHIFT1, SCALE1, GATE1, SHIFT2, SCALE2, GATE2 = range(N_MOD)


def _cparams(sem):
    return pltpu.CompilerParams(dimension_semantics=sem, vmem_limit_bytes=VMEM_LIMIT)


def _layer_spec(shape, l):
    return pl.BlockSpec((1,) + shape, lambda *_: (l,) + (0,) * len(shape))


def _mod_spec(l, who, slot):
    return pl.BlockSpec((1, BATCH, D_MODEL), lambda *_: (l, who, slot))


def _sigmoid(x):
    return 0.5 * jnp.tanh(0.5 * x) + 0.5


def _gelu(x):
    return 0.5 * x * (1.0 + jnp.tanh(0.7978845608028654 * (x + 0.044715 * (x * x * x))))


def _ln(x):
    mu = jnp.mean(x, axis=-1, keepdims=True)
    xc = x - mu
    var = jnp.mean(xc * xc, axis=-1, keepdims=True)
    return xc * lax.rsqrt(var + LN_EPS)


def _per_batch(x, fn):
    rows, d = x.shape
    return fn(x.reshape(rows // BATCH, BATCH, d)).reshape(rows, d)


def _modulate(x, shift, scale):
    return _per_batch(_ln(x), lambda h: h * (1.0 + scale)[None] + shift[None])


def _mod_kernel(c_ref, w_ref, b_ref, o_ref):
    c = c_ref[...]
    s = c * _sigmoid(c)
    o_ref[0] = jnp.dot(s, w_ref[0], preferred_element_type=F32) + b_ref[0]


def _mod_call(c_all, w_mod, b_mod):
    tn = 1536
    return pl.pallas_call(
        _mod_kernel,
        grid=(DEPTH, N_MOD * D_MODEL // tn),
        in_specs=[pl.BlockSpec((MOD_ROWS, D_MODEL), lambda l, j: (0, 0)),
                  pl.BlockSpec((1, D_MODEL, tn), lambda l, j: (l, 0, j)),
                  pl.BlockSpec((1, 1, tn), lambda l, j: (l, 0, j))],
        out_specs=pl.BlockSpec((1, MOD_ROWS, tn), lambda l, j: (l, 0, j)),
        out_shape=jax.ShapeDtypeStruct((DEPTH, MOD_ROWS, N_MOD * D_MODEL), F32),
        compiler_params=_cparams(("parallel", "parallel")),
        name="mod",
    )(c_all, w_mod, b_mod.reshape(DEPTH, 1, N_MOD * D_MODEL))


def _mix_in_kernel(x_ref, sh_ref, sc_ref, w_ref, u_ref, xl_ref, gl_ref):
    for r in range(x_ref.shape[0] // ROW_BLOCK):
        rs = slice(ROW_BLOCK * r, ROW_BLOCK * (r + 1))
        h = _modulate(x_ref[rs, :], sh_ref[0], sc_ref[0]).astype(BF16)
        p = jnp.dot(h, w_ref[0], preferred_element_type=F32)
        u_ref[rs, :] = p[:, :D_S5].astype(BF16)
        xl_ref[rs, :] = p[:, D_S5:D_S5 + D_LRU].astype(BF16)
        gl_ref[rs, :] = p[:, D_S5 + D_LRU:].astype(BF16)


def _mix_in_call(xs, mods, w_in, l, who):
    n = xs.shape[0]
    tm = 1024
    out = pl.BlockSpec((tm, D_S5), lambda i: (i, 0))
    return pl.pallas_call(
        _mix_in_kernel,
        grid=(n // tm,),
        in_specs=[pl.BlockSpec((tm, D_MODEL), lambda i: (i, 0)),
                  _mod_spec(l, who, SHIFT1), _mod_spec(l, who, SCALE1),
                  _layer_spec((D_MODEL, D_S5 + 2 * D_LRU), l)],
        out_specs=[out, out, out],
        out_shape=[jax.ShapeDtypeStruct((n, D_S5), BF16)] * 3,
        compiler_params=_cparams(("parallel",)),
        name="mix_in",
    )(xs, mods, mods, w_in)


def _s5_matrices(lam_re, lam_im, log_dt, b_re, b_im, c_re, c_im):
    T = S5_CHUNK
    lam = lax.complex(lam_re, lam_im)
    lam_dt = lam * jnp.exp(log_dt)[..., None]
    lam_bar = jnp.exp(lam_dt)
    b_bar = ((lam_bar - 1.0) / lam)[..., None] * lax.complex(b_re, b_im)
    c_mat = lax.complex(c_re, c_im)
    steps = jnp.arange(T + 1, dtype=F32)
    pw = jnp.exp(steps[:, None, None, None] * lam_dt[None])
    kd = jnp.real(jnp.einsum("dgop,ndgp,dgpi->ndgio", c_mat, pw[:T], b_bar))
    nb, gpb = S5_LANE_BLOCKS, GROUPS_PER_LANE_BLOCK
    half = gpb * S5_STATE

    lane_group = jnp.arange(LANES) // S5_GROUP_CH
    state_group = jnp.arange(half) // S5_STATE
    rep_ch = jnp.tile(jnp.eye(S5_GROUP_CH, dtype=BF16), (1, gpb))
    rep_st = jnp.tile(jnp.eye(S5_STATE, dtype=BF16), (1, gpb))
    same_ll = (lane_group[:, None] == lane_group[None, :]).astype(BF16)
    same_ls = (lane_group[:, None] == state_group[None, :]).astype(BF16)
    same_sl = same_ls.T

    def spread(x, rep, same, out):
        return jnp.einsum(out, x.astype(BF16), rep, preferred_element_type=BF16) * same

    ar = jnp.arange(T)
    kd = kd.at[0, 0].add(kd[0, 1])
    kb = spread(kd.reshape(T, 2, nb, LANES, S5_GROUP_CH), rep_ch, same_ll, "ndjxo,oq->dnjxq")
    m = jnp.stack([jnp.concatenate([kb[0, b - a] if b >= a else kb[1, a - b] for b in range(T)], axis=-1)
                   for a in range(T)], axis=1).reshape(nb, S5_TK, S5_TK)

    e_in = jnp.stack([T - 1 - ar, ar])
    pw_in = jnp.stack([pw[e_in[d], d] for d in range(2)])
    wc = (pw_in[:, :, :, None, :] * jnp.swapaxes(b_bar, -1, -2)[:, None]).reshape(2, T, nb, LANES, S5_STATE)
    w = jnp.concatenate([spread(part, rep_st, same_ls, "dtjxp,pq->djtxq")
                         for part in (jnp.real(wc), jnp.imag(wc))], axis=-1)
    w = w.reshape(2, nb, S5_TK, S5_STATE_COLS)

    f_out = jnp.stack([ar + 1, T - ar])
    pw_out = jnp.stack([pw[f_out[d], d] for d in range(2)])
    zc = (jnp.swapaxes(c_mat, -1, -2)[:, None] * pw_out[..., None]).reshape(2, T, nb, half, S5_GROUP_CH)
    z = jnp.concatenate([
        jnp.concatenate([spread(part[:, t], rep_ch, same_sl, "djyo,oq->djyq") for t in range(T)], axis=-1)
        for part in (jnp.real(zc), -jnp.imag(zc))], axis=-2)

    a_t = pw[T].reshape(2, nb, half)
    a = jnp.stack([jnp.real(a_t), jnp.imag(a_t)], axis=2)
    return w, m, z, a


def _s5_kernel(u_ref, w_ref, m_ref, z_ref, a_ref, init_ref, y_ref, fin_ref, s_ref, v_ref, spf_ref, spb_ref, *,
               kc, n_tiles):
    ph = pl.program_id(1)
    rt = pl.program_id(2)
    half = S5_STATE_COLS // 2
    rows = kc * BATCH

    @pl.when((ph == 0) & (rt == 0))
    def _():
        s_ref[...] = init_ref[:, 0]

    ucat = jnp.concatenate(
        [u_ref[:, BATCH * t:BATCH * (t + 1), :].reshape(rows, LANES) for t in range(S5_CHUNK)], axis=1)

    def recurrence(d, sp_ref, base):
        v_ref[...] = jnp.dot(ucat, w_ref[0, 0, 0], preferred_element_type=F32)
        a_re = jnp.broadcast_to(a_ref[0, d, 0, 0:1, :], (BATCH, half))
        a_im = jnp.broadcast_to(a_ref[0, d, 0, 1:2, :], (BATCH, half))

        def step(i, carry):
            s_re, s_im = carry
            off = pl.multiple_of((kc - 1 - i if d else i) * BATCH, BATCH)
            dst = pl.multiple_of(base + off, BATCH)
            sp_ref[pl.ds(dst, BATCH), :half] = s_re.astype(BF16)
            sp_ref[pl.ds(dst, BATCH), half:] = s_im.astype(BF16)
            v = v_ref[pl.ds(off, BATCH), :]
            n_re = a_re * s_re - a_im * s_im + v[:, :half]
            n_im = a_re * s_im + a_im * s_re + v[:, half:]
            return n_re, n_im

        s_re, s_im = lax.fori_loop(0, kc, step, (s_ref[d, :, :half], s_ref[d, :, half:]))
        s_ref[d, :, :half] = s_re
        s_ref[d, :, half:] = s_im

        @pl.when(rt == n_tiles - 1)
        def _():
            fin_ref[d, 0] = s_ref[d]

    @pl.when(ph == 0)
    def _():
        recurrence(1, spb_ref, pl.multiple_of((n_tiles - 1 - rt) * rows, rows))

    @pl.when(ph == 1)
    def _():
        recurrence(0, spf_ref, 0)
        s_bwd = spb_ref[pl.ds(pl.multiple_of(rt * rows, rows), rows), :]
        y = (jnp.dot(ucat, m_ref[0, 0], preferred_element_type=F32)
             + jnp.dot(spf_ref[...], z_ref[0, 0, 0], preferred_element_type=F32)
             + jnp.dot(s_bwd, z_ref[0, 1, 0], preferred_element_type=F32))
        for t in range(S5_CHUNK):
            y_ref[:, BATCH * t:BATCH * (t + 1), :] = (
                y[:, LANES * t:LANES * (t + 1)].reshape(kc, BATCH, LANES).astype(BF16))


def _s5_call(u, mats, init, l):
    w, m, z, a = mats
    n = u.shape[0]
    chunk_rows = S5_CHUNK * BATCH
    n_chunks = n // chunk_rows
    kc = min(32, n_chunks)
    n_tiles = n_chunks // kc
    nb = S5_LANE_BLOCKS

    def tile(ph, rt):
        return rt + (1 - ph) * (n_tiles - 1 - 2 * rt)

    both_dirs = lambda shape: pl.BlockSpec((2, 1) + shape, lambda j, ph, rt: (0, j, 0, 0))
    y, fin = pl.pallas_call(
        functools.partial(_s5_kernel, kc=kc, n_tiles=n_tiles),
        grid=(nb, 2, n_tiles),
        in_specs=[pl.BlockSpec((kc, chunk_rows, LANES), lambda j, ph, rt: (tile(ph, rt), 0, j)),
                  pl.BlockSpec((1, 1, 1, S5_TK, S5_STATE_COLS), lambda j, ph, rt: (l, 1 - ph, j, 0, 0)),
                  pl.BlockSpec((1, 1, S5_TK, S5_TK), lambda j, ph, rt: (l, j, 0, 0)),
                  pl.BlockSpec((1, 2, 1, S5_STATE_COLS, S5_TK), lambda j, ph, rt: (l, 0, j, 0, 0)),
                  pl.BlockSpec((1, 2, 1, 2, S5_STATE_COLS // 2), lambda j, ph, rt: (l, 0, j, 0, 0)),
                  both_dirs((BATCH, S5_STATE_COLS))],
        out_specs=[pl.BlockSpec((kc, chunk_rows, LANES), lambda j, ph, rt: (ph * rt, 0, j)),
                   both_dirs((BATCH, S5_STATE_COLS))],
        out_shape=[jax.ShapeDtypeStruct((n_chunks, chunk_rows, D_S5), BF16),
                   jax.ShapeDtypeStruct((2, nb, BATCH, S5_STATE_COLS), F32)],
        scratch_shapes=[pltpu.VMEM((2, BATCH, S5_STATE_COLS), F32),
                        pltpu.VMEM((kc * BATCH, S5_STATE_COLS), F32),
                        pltpu.VMEM((kc * BATCH, S5_STATE_COLS), BF16),
                        pltpu.VMEM((n_chunks * BATCH, S5_STATE_COLS), BF16)],
        compiler_params=_cparams(("arbitrary", "arbitrary", "arbitrary")),
        name="s5",
    )(u.reshape(n_chunks, chunk_rows, D_S5), w, m, z, a, init)
    return y.reshape(n, D_S5), fin


LRU_HALO_PREV = 2 * BATCH
LRU_HALO_NEXT = BATCH


def _lru_kernel(xp_ref, xc_ref, xn_ref, cw_ref, cb_ref, wg_ref, bg_ref, lam_ref, init_ref,
                h_ref, fin_ref, xpad_ref, a_ref, b_ref, hst_ref, *, tm, n_tiles):
    d = pl.program_id(0)
    i = pl.program_id(1)
    te = i + d * (n_tiles - 1 - 2 * i)

    @pl.when(i == 0)
    def _():
        hst_ref[...] = init_ref[0]

    xpad_ref[0:LRU_HALO_PREV, :] = jnp.where(te > 0, xp_ref[...].astype(F32), 0.0)
    xpad_ref[LRU_HALO_PREV:LRU_HALO_PREV + tm, :] = xc_ref[...].astype(F32)
    xpad_ref[LRU_HALO_PREV + tm:, :] = jnp.where(te < n_tiles - 1, xn_ref[...].astype(F32), 0.0)

    xc = cb_ref[0]
    for k in range(4):
        xc = xc + cw_ref[0, k:k + 1, :] * xpad_ref[BATCH * k:BATCH * k + tm, :]

    gates = jnp.tanh(jnp.dot(xc.astype(BF16), wg_ref[0, 0], preferred_element_type=F32) + bg_ref[0, 0])
    z = -lam_ref[0, 0]
    rate = (-0.5 * LRU_C) * (jnp.maximum(z, 0.0) + jnp.log1p(jnp.exp(-jnp.abs(z))))
    log_a = rate * gates[:, :D_LRU] + rate
    a = jnp.exp(log_a)
    half_x = 0.5 * xc
    gated_x = half_x * gates[:, D_LRU:] + half_x
    a_ref[...] = a
    b_ref[...] = jnp.sqrt(-jnp.tanh(log_a) * (a * a + 1.0)) * gated_x

    steps = tm // BATCH

    def step(j, h):
        k = j + d * (steps - 1 - 2 * j)
        off = pl.multiple_of(k * BATCH, BATCH)
        h = a_ref[pl.ds(off, BATCH), :] * h + b_ref[pl.ds(off, BATCH), :]
        b_ref[pl.ds(off, BATCH), :] = h
        return h

    h = lax.fori_loop(0, steps, step, hst_ref[...])
    hst_ref[...] = h
    h_ref[0] = b_ref[...].astype(BF16)

    @pl.when(i == n_tiles - 1)
    def _():
        fin_ref[0] = h


def _lru_call(xl, conv_w, conv_b, wg, bg, lam, init, l):
    n = xl.shape[0]
    tm = 2048
    n_tiles = n // tm

    def tile(d, i):
        return i + d * (n_tiles - 1 - 2 * i)

    prev_blocks = tm // LRU_HALO_PREV
    next_blocks = tm // LRU_HALO_NEXT
    layer_dir = lambda shape: pl.BlockSpec((1, 1) + shape, lambda d, i: (l, d) + (0,) * len(shape))
    per_dir = lambda shape: pl.BlockSpec((1,) + shape, lambda d, i: (d,) + (0,) * len(shape))
    return pl.pallas_call(
        functools.partial(_lru_kernel, tm=tm, n_tiles=n_tiles),
        grid=(2, n_tiles),
        in_specs=[pl.BlockSpec((LRU_HALO_PREV, D_LRU),
                               lambda d, i: (jnp.maximum(tile(d, i) * prev_blocks - 1, 0), 0)),
                  pl.BlockSpec((tm, D_LRU), lambda d, i: (tile(d, i), 0)),
                  pl.BlockSpec((LRU_HALO_NEXT, D_LRU),
                               lambda d, i: (jnp.minimum((tile(d, i) + 1) * next_blocks, n // LRU_HALO_NEXT - 1), 0)),
                  _layer_spec((4, D_LRU), l), _layer_spec((1, D_LRU), l),
                  layer_dir((D_LRU, 2 * D_LRU)), layer_dir((1, 2 * D_LRU)), layer_dir((1, D_LRU)),
                  per_dir((BATCH, D_LRU))],
        out_specs=[pl.BlockSpec((1, tm, D_LRU), lambda d, i: (d, tile(d, i), 0)),
                   per_dir((BATCH, D_LRU))],
        out_shape=[jax.ShapeDtypeStruct((2, n, D_LRU), BF16),
                   jax.ShapeDtypeStruct((2, BATCH, D_LRU), F32)],
        scratch_shapes=[pltpu.VMEM((LRU_HALO_PREV + tm + LRU_HALO_NEXT, D_LRU), F32),
                        pltpu.VMEM((tm, D_LRU), F32),
                        pltpu.VMEM((tm, D_LRU), F32),
                        pltpu.VMEM((BATCH, D_LRU), F32)],
        compiler_params=_cparams(("arbitrary", "arbitrary")),
        name="lru",
    )(xl, xl, xl, conv_w, conv_b, wg, bg, lam, init)


def _mix_out_kernel(ys_ref, u_ref, hl_ref, gl_ref, xs_ref, d_ref, wglu_ref, bglu_ref, wout_ref,
                    g1_ref, lng_ref, lnb_ref, sh2_ref, sc2_ref, xo_ref, h2_ref):
    g1 = g1_ref[0]
    for r in range(xs_ref.shape[0] // ROW_BLOCK):
        rs = slice(ROW_BLOCK * r, ROW_BLOCK * (r + 1))
        y = (ys_ref[rs, :].astype(F32) + d_ref[0] * u_ref[rs, :].astype(F32)).astype(BF16)
        y = _gelu(y)
        glu = jnp.dot(y, wglu_ref[0], preferred_element_type=F32) + bglu_ref[0]
        y = y * _sigmoid(glu.astype(BF16))
        y_lru = (hl_ref[0, rs, :] + hl_ref[1, rs, :]) * _gelu(gl_ref[rs, :])
        out = jnp.dot(y, wout_ref[0, :D_S5, :], preferred_element_type=F32)
        out = out + jnp.dot(y_lru, wout_ref[0, D_S5:, :], preferred_element_type=F32)
        zsum = DEEPNORM_ALPHA * xs_ref[rs, :] + _per_batch(out, lambda o: o * g1[None])
        xs_new = _ln(zsum) * lng_ref[0] + lnb_ref[0]
        xo_ref[rs, :] = xs_new
        h2_ref[rs, :] = _modulate(xs_new, sh2_ref[0], sc2_ref[0]).astype(BF16)


def _mix_out_call(ys, u, hl, gl, xs, mods, s5_d, w_glu, b_glu, w_out, ln_g, ln_b, l, who):
    n = xs.shape[0]
    tm = 1024
    half = pl.BlockSpec((tm, D_S5), lambda i: (i, 0))
    both = pl.BlockSpec((2, tm, D_S5), lambda i: (0, i, 0))
    row = pl.BlockSpec((tm, D_MODEL), lambda i: (i, 0))
    return pl.pallas_call(
        _mix_out_kernel,
        grid=(n // tm,),
        in_specs=[half, half, both, half, row,
                  _layer_spec((1, D_S5), l), _layer_spec((D_S5, D_S5), l), _layer_spec((1, D_S5), l),
                  _layer_spec((D_MODEL, D_MODEL), l), _mod_spec(l, who, GATE1),
                  _layer_spec((1, D_MODEL), l), _layer_spec((1, D_MODEL), l),
                  _mod_spec(l, who, SHIFT2), _mod_spec(l, who, SCALE2)],
        out_specs=[row, row],
        out_shape=[jax.ShapeDtypeStruct((n, D_MODEL), F32), jax.ShapeDtypeStruct((n, D_MODEL), BF16)],
        compiler_params=_cparams(("parallel",)),
        name="mix_out",
    )(ys, u, hl, gl, xs, s5_d, w_glu, b_glu, w_out, mods, ln_g, ln_b, mods, mods)


def _ffn_up_kernel(h_ref, w_ref, o_ref):
    o_ref[0] = jnp.dot(h_ref[...], w_ref[0, 0], preferred_element_type=F32).astype(BF16)


def _ffn_up_call(h2, w_up, l):
    n = h2.shape[0]
    tm = 1024
    return pl.pallas_call(
        _ffn_up_kernel,
        grid=(2, n // tm),
        in_specs=[pl.BlockSpec((tm, D_MODEL), lambda s, i: (i, 0)),
                  pl.BlockSpec((1, 1, D_MODEL, D_FF), lambda s, i: (l, s, 0, 0))],
        out_specs=pl.BlockSpec((1, tm, D_FF), lambda s, i: (s, i, 0)),
        out_shape=jax.ShapeDtypeStruct((2, n, D_FF), BF16),
        compiler_params=_cparams(("parallel", "parallel")),
        name="ffn_up",
    )(h2, w_up)


GRID_ROW = GRID_W * BATCH
EPILOGUE_ROWS = 256
CONV_ROWS = 512


def _ffn_down_kernel(u_ref, v_ref, cw_ref, cb_ref, wd_ref, xs_ref, g2_ref, lng_ref, lnb_ref,
                     xo_ref, win_ref, act_ref, wb_ref, *, n_tiles, on_grid):
    s = pl.program_id(0)
    c = pl.program_id(1)
    tm = GRID_ROW
    win = win_ref.at[c]

    @pl.when(s == 0)
    def _():
        win[0] = jnp.zeros((tm, FF_CHUNK), BF16)

    def finish_tile(accumulate):
        live = (s < n_tiles).astype(F32)
        for t in range(9):
            tap = cw_ref[0, t:t + 1, :] * live if (on_grid and t >= 6) else cw_ref[0, t:t + 1, :]
            wb_ref[t] = jnp.broadcast_to(tap, (BATCH, FF_CHUNK)).astype(BF16)
        wb_ref[9] = jnp.broadcast_to(cb_ref[0], (BATCH, FF_CHUNK)).astype(BF16)
        edge = jnp.zeros((BATCH, LANES), BF16)
        nb16 = CONV_ROWS // BATCH
        for r in range(tm // CONV_ROWS):
            r0, r1 = CONV_ROWS * r, CONV_ROWS * (r + 1)
            for k in range(FF_CHUNK // LANES):
                ls = slice(LANES * k, LANES * (k + 1))
                if on_grid:
                    slabs = ((-1, lambda lo, hi: win[0, lo:hi, ls], edge, edge),
                             (0, lambda lo, hi: win[1, lo:hi, ls], edge, edge),
                             (1, lambda lo, hi: u_ref[0, lo:hi, ls], edge, edge))
                else:
                    after = jnp.where(s < n_tiles, u_ref[0, :BATCH, ls], jnp.zeros((), BF16))
                    slabs = ((0, lambda lo, hi: win[1, lo:hi, ls], win[0, tm - BATCH:, ls], after),)
                conv = wb_ref[9, :, ls][None]
                for dy, rows, before, after in slabs:
                    mid = rows(r0, r1)
                    left = rows(r0 - BATCH, r1 - BATCH) if r0 else jnp.concatenate([before, rows(0, r1 - BATCH)], axis=0)
                    right = (rows(r0 + BATCH, r1 + BATCH) if r1 < tm
                             else jnp.concatenate([rows(r0 + BATCH, tm), after], axis=0))
                    t0 = 3 * (dy + 1)
                    for t, x in ((t0, left), (t0 + 1, mid), (t0 + 2, right)):
                        conv = conv + wb_ref[t, :, ls][None] * x.reshape(nb16, BATCH, LANES)
                gate = v_ref[0, r0:r1, ls].reshape(nb16, BATCH, LANES)
                act_ref[r0:r1, ls] = (_gelu(conv) * gate).reshape(CONV_ROWS, LANES)
            part = jnp.dot(act_ref[r0:r1, :], wd_ref[0], preferred_element_type=F32)
            if accumulate:
                xo_ref[r0:r1, :] += part
            else:
                xo_ref[r0:r1, :] = part
        win[0] = win[1]

    @pl.when((s > 0) & (c == 0))
    def _():
        finish_tile(False)

    @pl.when((s > 0) & (c == 1))
    def _():
        finish_tile(True)
        g2 = g2_ref[0]
        for r in range(tm // EPILOGUE_ROWS):
            rs = slice(EPILOGUE_ROWS * r, EPILOGUE_ROWS * (r + 1))
            zsum = DEEPNORM_ALPHA * xs_ref[rs, :] + _per_batch(xo_ref[rs, :], lambda o: o * g2[None])
            xo_ref[rs, :] = _ln(zsum) * lng_ref[0] + lnb_ref[0]

    win[1] = u_ref[0]


def _ffn_down_call(uv, xs, mods, conv_w, conv_b, w_down, ln_g, ln_b, l, who, on_grid):
    n = xs.shape[0]
    tm = GRID_ROW
    n_tiles = n // tm
    n_chunks = D_FF // FF_CHUNK
    assert n_chunks == 2
    done = lambda s: jnp.maximum(s - 1, 0)
    row = pl.BlockSpec((tm, D_MODEL), lambda s, c: (done(s), 0))
    return pl.pallas_call(
        functools.partial(_ffn_down_kernel, n_tiles=n_tiles, on_grid=on_grid),
        grid=(n_tiles + 1, n_chunks),
        in_specs=[pl.BlockSpec((1, tm, FF_CHUNK), lambda s, c: (0, jnp.minimum(s, n_tiles - 1), c)),
                  pl.BlockSpec((1, tm, FF_CHUNK), lambda s, c: (1, done(s), c)),
                  pl.BlockSpec((1, 9, FF_CHUNK), lambda s, c: (l, 0, c)),
                  pl.BlockSpec((1, 1, FF_CHUNK), lambda s, c: (l, 0, c)),
                  pl.BlockSpec((1, FF_CHUNK, D_MODEL), lambda s, c: (l, c, 0)),
                  row, _mod_spec(l, who, GATE2), _layer_spec((1, D_MODEL), l), _layer_spec((1, D_MODEL), l)],
        out_specs=row,
        out_shape=jax.ShapeDtypeStruct((n, D_MODEL), F32),
        scratch_shapes=[pltpu.VMEM((n_chunks, 2, tm, FF_CHUNK), BF16),
                        pltpu.VMEM((tm, FF_CHUNK), BF16),
                        pltpu.VMEM((10, BATCH, FF_CHUNK), BF16)],
        compiler_params=_cparams(("arbitrary", "arbitrary")),
        name="ffn_down_grid" if on_grid else "ffn_down_seq",
    )(uv, uv, conv_w, conv_b, w_down, xs, mods, ln_g, ln_b)


def _time_major(x):
    b, length, d = x.shape
    return x.transpose(1, 0, 2).reshape(length * b, d)


def kernel(x, c, ctx, c_ctx, w_mod, b_mod, w_in, s5_lam_re, s5_lam_im, s5_log_dt, s5_b_re, s5_b_im, s5_c_re, s5_c_im, s5_d, s5_w_glu, s5_b_glu, lru_conv_w, lru_conv_b, lru_w_a, lru_b_a, lru_w_x, lru_b_x, lru_lam, w_out, ln1_g, ln1_b, ffn_w_up, ffn_conv_w, ffn_conv_b, ffn_w_down, ln2_g, ln2_b):
    bsz, seq, _ = x.shape
    xs = _time_major(x)
    cs = _time_major(ctx)

    c_all = jnp.concatenate([c, jnp.broadcast_to(c_ctx, (BATCH, D_MODEL))], axis=0)
    mods = _mod_call(c_all, w_mod, b_mod)

    row = lambda v: v.reshape(DEPTH, 1, v.shape[-1])
    w_in_b = w_in.astype(BF16)
    mats = jax.vmap(_s5_matrices)(s5_lam_re, s5_lam_im, s5_log_dt, s5_b_re, s5_b_im, s5_c_re, s5_c_im)
    eye_h = jnp.eye(LRU_HEADS, dtype=F32)
    dense = lambda wh: jnp.einsum("ldhij,hk->ldhikj", wh, eye_h).reshape(DEPTH, 2, D_LRU, D_LRU)
    wg = (0.5 * jnp.concatenate([dense(lru_w_a), dense(lru_w_x)], axis=-1)).astype(BF16)
    bg = 0.5 * jnp.concatenate([lru_b_a, lru_b_x], axis=-1).reshape(DEPTH, 2, 1, 2 * D_LRU)
    lam = lru_lam.reshape(DEPTH, 2, 1, D_LRU)
    w_glu = s5_w_glu.astype(BF16)
    w_out_b = w_out.astype(BF16)
    w_up = ffn_w_up.reshape(DEPTH, D_MODEL, 2, D_FF).transpose(0, 2, 1, 3).astype(BF16)
    f_conv_w = ffn_conv_w.reshape(DEPTH, 9, D_FF)
    w_down = ffn_w_down.astype(BF16)
    ln1 = (row(ln1_g), row(ln1_b))
    ln2 = (row(ln2_g), row(ln2_b))

    s5_zero = jnp.zeros((2, S5_LANE_BLOCKS, BATCH, S5_STATE_COLS), F32)
    lru_zero = jnp.zeros((2, BATCH, D_LRU), F32)

    for l in range(DEPTH):
        last = l == DEPTH - 1

        def mixers(stream, who, s5_init, lru_init):
            u, xl, gl = _mix_in_call(stream, mods, w_in_b, l, who)
            ys, s5_fin = _s5_call(u, mats, s5_init, l)
            hl, lru_fin = _lru_call(xl, lru_conv_w, row(lru_conv_b), wg, bg, lam, lru_init, l)
            return (ys, u, hl, gl), (s5_fin, lru_fin)

        def finish(stream, mixed, who, on_grid):
            xs_mid, h2 = _mix_out_call(*mixed, stream, mods, row(s5_d), w_glu, row(s5_b_glu), w_out_b, *ln1, l, who)
            uv = _ffn_up_call(h2, w_up, l)
            return _ffn_down_call(uv, xs_mid, mods, f_conv_w, row(ffn_conv_b), w_down, *ln2, l, who, on_grid)

        mixed_c, (s5_fin, lru_fin) = mixers(cs, 1, s5_zero, lru_zero)
        mixed_x, _ = mixers(xs, 0, s5_fin, lru_fin)
        xs = finish(xs, mixed_x, 0, True)
        if not last:
            cs = finish(cs, mixed_c, 1, False)

    return xs.reshape(seq, bsz, D_MODEL).transpose(1, 0, 2)
```

```python
import functools

import jax
import jax.numpy as jnp
from jax import lax
from jax.experimental import pallas as pl
from jax.experimental.pallas import tpu as pltpu

F32 = jnp.float32
BF16 = jnp.bfloat16

D_MODEL = 1024
BATCH = 16
DEPTH = 4
GRID_W = 64
D_S5 = 512
D_LRU = 512
S5_GROUP_CH = 16
S5_GROUPS = 32
S5_STATE = 64
LRU_HEADS = 8
LRU_HEAD_DIM = 64
LRU_C = 8.0
D_FF = 2816
N_MOD = 6
LN_EPS = 1e-6
DEEPNORM_ALPHA = (2 * DEPTH) ** 0.25

LANES = 128
GROUPS_PER_LANE_BLOCK = LANES // S5_GROUP_CH
S5_LANE_BLOCKS = D_S5 // LANES
S5_STATE_COLS = 2 * GROUPS_PER_LANE_BLOCK * S5_STATE
S5_CHUNK = 8
S5_TK = S5_CHUNK * LANES
MOD_ROWS = 2 * BATCH
ROW_BLOCK = 256
FF_CHUNK = 1408
VMEM_LIMIT = 56 * 1024 * 1024
SHIFT1, SCALE1, GATE1, SHIFT2, SCALE2, GATE2 = range(N_MOD)


def _cparams(sem):
    return pltpu.CompilerParams(dimension_semantics=sem, vmem_limit_bytes=VMEM_LIMIT)


def _layer_spec(shape, l):
    return pl.BlockSpec((1,) + shape, lambda *_: (l,) + (0,) * len(shape))


def _mod_spec(l, who, slot):
    return pl.BlockSpec((1, BATCH, D_MODEL), lambda *_: (l, who, slot))


def _sigmoid(x):
    return 0.5 * jnp.tanh(0.5 * x) + 0.5


def _gelu(x):
    return 0.5 * x * (1.0 + jnp.tanh(0.7978845608028654 * (x + 0.044715 * (x * x * x))))


def _ln(x):
    mu = jnp.mean(x, axis=-1, keepdims=True)
    xc = x - mu
    var = jnp.mean(xc * xc, axis=-1, keepdims=True)
    return xc * lax.rsqrt(var + LN_EPS)


def _per_batch(x, fn):
    rows, d = x.shape
    return fn(x.reshape(rows // BATCH, BATCH, d)).reshape(rows, d)


def _modulate(x, shift, scale):
    return _per_batch(_ln(x), lambda h: h * (1.0 + scale)[None] + shift[None])


def _mod_kernel(c_ref, w_ref, b_ref, o_ref):
    c = c_ref[...]
    s = c * _sigmoid(c)
    o_ref[0] = jnp.dot(s, w_ref[0], preferred_element_type=F32) + b_ref[0]


def _mod_call(c_all, w_mod, b_mod):
    tn = 1536
    return pl.pallas_call(
        _mod_kernel,
        grid=(DEPTH, N_MOD * D_MODEL // tn),
        in_specs=[pl.BlockSpec((MOD_ROWS, D_MODEL), lambda l, j: (0, 0)),
                  pl.BlockSpec((1, D_MODEL, tn), lambda l, j: (l, 0, j)),
                  pl.BlockSpec((1, 1, tn), lambda l, j: (l, 0, j))],
        out_specs=pl.BlockSpec((1, MOD_ROWS, tn), lambda l, j: (l, 0, j)),
        out_shape=jax.ShapeDtypeStruct((DEPTH, MOD_ROWS, N_MOD * D_MODEL), F32),
        compiler_params=_cparams(("parallel", "parallel")),
        name="mod",
    )(c_all, w_mod, b_mod.reshape(DEPTH, 1, N_MOD * D_MODEL))


def _mix_in_kernel(x_ref, sh_ref, sc_ref, w_ref, u_ref, xl_ref, gl_ref, *, rb):
    for r in range(x_ref.shape[0] // rb):
        rs = slice(rb * r, rb * (r + 1))
        h = _modulate(x_ref[rs, :], sh_ref[0], sc_ref[0]).astype(BF16)
        p = jnp.dot(h, w_ref[0], preferred_element_type=F32)
        u_ref[rs, :] = p[:, :D_S5].astype(BF16)
        xl_ref[rs, :] = p[:, D_S5:D_S5 + D_LRU].astype(BF16)
        gl_ref[rs, :] = p[:, D_S5 + D_LRU:].astype(BF16)


def _mix_in_call(xs, mods, w_in, l, who):
    n = xs.shape[0]
    tm = 1024
    out = pl.BlockSpec((tm, D_S5), lambda i: (i, 0))
    return pl.pallas_call(
        functools.partial(_mix_in_kernel, rb=(ROW_BLOCK, ROW_BLOCK, 512, 128)[l]),
        grid=(n // tm,),
        in_specs=[pl.BlockSpec((tm, D_MODEL), lambda i: (i, 0)),
                  _mod_spec(l, who, SHIFT1), _mod_spec(l, who, SCALE1),
                  _layer_spec((D_MODEL, D_S5 + 2 * D_LRU), l)],
        out_specs=[out, out, out],
        out_shape=[jax.ShapeDtypeStruct((n, D_S5), BF16)] * 3,
        compiler_params=_cparams(("parallel",)),
        name="mix_in",
    )(xs, mods, mods, w_in)


def _s5_matrices(lam_re, lam_im, log_dt, b_re, b_im, c_re, c_im):
    T = S5_CHUNK
    lam = lax.complex(lam_re, lam_im)
    lam_dt = lam * jnp.exp(log_dt)[..., None]
    lam_bar = jnp.exp(lam_dt)
    b_bar = ((lam_bar - 1.0) / lam)[..., None] * lax.complex(b_re, b_im)
    c_mat = lax.complex(c_re, c_im)
    steps = jnp.arange(T + 1, dtype=F32)
    pw = jnp.exp(steps[:, None, None, None] * lam_dt[None])
    kd = jnp.real(jnp.einsum("dgop,ndgp,dgpi->ndgio", c_mat, pw[:T], b_bar))
    nb, gpb = S5_LANE_BLOCKS, GROUPS_PER_LANE_BLOCK
    half = gpb * S5_STATE

    lane_group = jnp.arange(LANES) // S5_GROUP_CH
    state_group = jnp.arange(half) // S5_STATE
    rep_ch = jnp.tile(jnp.eye(S5_GROUP_CH, dtype=BF16), (1, gpb))
    rep_st = jnp.tile(jnp.eye(S5_STATE, dtype=BF16), (1, gpb))
    same_ll = (lane_group[:, None] == lane_group[None, :]).astype(BF16)
    same_ls = (lane_group[:, None] == state_group[None, :]).astype(BF16)
    same_sl = same_ls.T

    def spread(x, rep, same, out):
        return jnp.einsum(out, x.astype(BF16), rep, preferred_element_type=BF16) * same

    ar = jnp.arange(T)
    kd = kd.at[0, 0].add(kd[0, 1])
    kb = spread(kd.reshape(T, 2, nb, LANES, S5_GROUP_CH), rep_ch, same_ll, "ndjxo,oq->dnjxq")
    m = jnp.stack([jnp.concatenate([kb[0, b - a] if b >= a else kb[1, a - b] for b in range(T)], axis=-1)
                   for a in range(T)], axis=1).reshape(nb, S5_TK, S5_TK)

    e_in = jnp.stack([T - 1 - ar, ar])
    pw_in = jnp.stack([pw[e_in[d], d] for d in range(2)])
    wc = (pw_in[:, :, :, None, :] * jnp.swapaxes(b_bar, -1, -2)[:, None]).reshape(2, T, nb, LANES, S5_STATE)
    w = jnp.concatenate([spread(part, rep_st, same_ls, "dtjxp,pq->djtxq")
                         for part in (jnp.real(wc), jnp.imag(wc))], axis=-1)
    w = w.reshape(2, nb, S5_TK, S5_STATE_COLS)

    f_out = jnp.stack([ar + 1, T - ar])
    pw_out = jnp.stack([pw[f_out[d], d] for d in range(2)])
    zc = (jnp.swapaxes(c_mat, -1, -2)[:, None] * pw_out[..., None]).reshape(2, T, nb, half, S5_GROUP_CH)
    z = jnp.concatenate([
        jnp.concatenate([spread(part[:, t], rep_ch, same_sl, "djyo,oq->djyq") for t in range(T)], axis=-1)
        for part in (jnp.real(zc), -jnp.imag(zc))], axis=-2)

    a_t = pw[T].reshape(2, nb, half)
    a = jnp.stack([jnp.real(a_t), jnp.imag(a_t)], axis=2)
    return w, m, z, a


def _s5_kernel(u_ref, w_ref, m_ref, z_ref, a_ref, init_ref, y_ref, fin_ref, s_ref, v_ref, spf_ref, spb_ref, *,
               kc, n_tiles):
    ph = pl.program_id(1)
    rt = pl.program_id(2)
    half = S5_STATE_COLS // 2
    rows = kc * BATCH

    @pl.when((ph == 0) & (rt == 0))
    def _():
        s_ref[...] = init_ref[:, 0]

    ucat = jnp.concatenate(
        [u_ref[:, BATCH * t:BATCH * (t + 1), :].reshape(rows, LANES) for t in range(S5_CHUNK)], axis=1)

    def recurrence(d, sp_ref, base):
        v_ref[...] = jnp.dot(ucat, w_ref[0, 0, 0], preferred_element_type=F32)
        a_re = jnp.broadcast_to(a_ref[0, d, 0, 0:1, :], (BATCH, half))
        a_im = jnp.broadcast_to(a_ref[0, d, 0, 1:2, :], (BATCH, half))

        def step(i, carry):
            s_re, s_im = carry
            off = pl.multiple_of((kc - 1 - i if d else i) * BATCH, BATCH)
            dst = pl.multiple_of(base + off, BATCH)
            sp_ref[pl.ds(dst, BATCH), :half] = s_re.astype(BF16)
            sp_ref[pl.ds(dst, BATCH), half:] = s_im.astype(BF16)
            v = v_ref[pl.ds(off, BATCH), :]
            n_re = a_re * s_re - a_im * s_im + v[:, :half]
            n_im = a_re * s_im + a_im * s_re + v[:, half:]
            return n_re, n_im

        s_re, s_im = lax.fori_loop(0, kc, step, (s_ref[d, :, :half], s_ref[d, :, half:]))
        s_ref[d, :, :half] = s_re
        s_ref[d, :, half:] = s_im

        @pl.when(rt == n_tiles - 1)
        def _():
            fin_ref[d, 0] = s_ref[d]

    @pl.when(ph == 0)
    def _():
        recurrence(1, spb_ref, pl.multiple_of((n_tiles - 1 - rt) * rows, rows))

    @pl.when(ph == 1)
    def _():
        recurrence(0, spf_ref, 0)
        s_bwd = spb_ref[pl.ds(pl.multiple_of(rt * rows, rows), rows), :]
        y = (jnp.dot(ucat, m_ref[0, 0], preferred_element_type=F32)
             + jnp.dot(spf_ref[...], z_ref[0, 0, 0], preferred_element_type=F32)
             + jnp.dot(s_bwd, z_ref[0, 1, 0], preferred_element_type=F32))
        for t in range(S5_CHUNK):
            y_ref[:, BATCH * t:BATCH * (t + 1), :] = (
                y[:, LANES * t:LANES * (t + 1)].reshape(kc, BATCH, LANES).astype(BF16))


def _s5_call(u, mats, init, l):
    w, m, z, a = mats
    n = u.shape[0]
    chunk_rows = S5_CHUNK * BATCH
    n_chunks = n // chunk_rows
    kc = min(32, n_chunks)
    n_tiles = n_chunks // kc
    nb = S5_LANE_BLOCKS

    def tile(ph, rt):
        return rt + (1 - ph) * (n_tiles - 1 - 2 * rt)

    both_dirs = lambda shape: pl.BlockSpec((2, 1) + shape, lambda j, ph, rt: (0, j, 0, 0))
    y, fin = pl.pallas_call(
        functools.partial(_s5_kernel, kc=kc, n_tiles=n_tiles),
        grid=(nb, 2, n_tiles),
        in_specs=[pl.BlockSpec((kc, chunk_rows, LANES), lambda j, ph, rt: (tile(ph, rt), 0, j)),
                  pl.BlockSpec((1, 1, 1, S5_TK, S5_STATE_COLS), lambda j, ph, rt: (l, 1 - ph, j, 0, 0)),
                  pl.BlockSpec((1, 1, S5_TK, S5_TK), lambda j, ph, rt: (l, j, 0, 0)),
                  pl.BlockSpec((1, 2, 1, S5_STATE_COLS, S5_TK), lambda j, ph, rt: (l, 0, j, 0, 0)),
                  pl.BlockSpec((1, 2, 1, 2, S5_STATE_COLS // 2), lambda j, ph, rt: (l, 0, j, 0, 0)),
                  both_dirs((BATCH, S5_STATE_COLS))],
        out_specs=[pl.BlockSpec((kc, chunk_rows, LANES), lambda j, ph, rt: (ph * rt, 0, j)),
                   both_dirs((BATCH, S5_STATE_COLS))],
        out_shape=[jax.ShapeDtypeStruct((n_chunks, chunk_rows, D_S5), BF16),
                   jax.ShapeDtypeStruct((2, nb, BATCH, S5_STATE_COLS), F32)],
        scratch_shapes=[pltpu.VMEM((2, BATCH, S5_STATE_COLS), F32),
                        pltpu.VMEM((kc * BATCH, S5_STATE_COLS), F32),
                        pltpu.VMEM((kc * BATCH, S5_STATE_COLS), BF16),
                        pltpu.VMEM((n_chunks * BATCH, S5_STATE_COLS), BF16)],
        compiler_params=_cparams(("arbitrary", "arbitrary", "arbitrary")),
        name="s5",
    )(u.reshape(n_chunks, chunk_rows, D_S5), w, m, z, a, init)
    return y.reshape(n, D_S5), fin


LRU_HALO_PREV = 2 * BATCH
LRU_HALO_NEXT = BATCH


def _lru_kernel(xp_ref, xc_ref, xn_ref, cw_ref, cb_ref, wg_ref, bg_ref, lam_ref, init_ref,
                h_ref, fin_ref, xpad_ref, a_ref, b_ref, hst_ref, *, tm, n_tiles):
    d = pl.program_id(0)
    i = pl.program_id(1)
    te = i + d * (n_tiles - 1 - 2 * i)

    @pl.when(i == 0)
    def _():
        hst_ref[...] = init_ref[0]

    xpad_ref[0:LRU_HALO_PREV, :] = jnp.where(te > 0, xp_ref[...].astype(F32), 0.0)
    xpad_ref[LRU_HALO_PREV:LRU_HALO_PREV + tm, :] = xc_ref[...].astype(F32)
    xpad_ref[LRU_HALO_PREV + tm:, :] = jnp.where(te < n_tiles - 1, xn_ref[...].astype(F32), 0.0)

    xc = cb_ref[0]
    for k in range(4):
        xc = xc + cw_ref[0, k:k + 1, :] * xpad_ref[BATCH * k:BATCH * k + tm, :]

    gates = jnp.tanh(jnp.dot(xc.astype(BF16), wg_ref[0, 0], preferred_element_type=F32) + bg_ref[0, 0])
    z = -lam_ref[0, 0]
    rate = (-0.5 * LRU_C) * (jnp.maximum(z, 0.0) + jnp.log1p(jnp.exp(-jnp.abs(z))))
    log_a = rate * gates[:, :D_LRU] + rate
    a = jnp.exp(log_a)
    half_x = 0.5 * xc
    gated_x = half_x * gates[:, D_LRU:] + half_x
    a_ref[...] = a
    b_ref[...] = jnp.sqrt(-jnp.tanh(log_a) * (a * a + 1.0)) * gated_x

    steps = tm // BATCH

    def step(j, h):
        k = j + d * (steps - 1 - 2 * j)
        off = pl.multiple_of(k * BATCH, BATCH)
        h = a_ref[pl.ds(off, BATCH), :] * h + b_ref[pl.ds(off, BATCH), :]
        b_ref[pl.ds(off, BATCH), :] = h
        return h

    h = lax.fori_loop(0, steps, step, hst_ref[...])
    hst_ref[...] = h
    h_ref[0] = b_ref[...].astype(BF16)

    @pl.when(i == n_tiles - 1)
    def _():
        fin_ref[0] = h


def _lru_call(xl, conv_w, conv_b, wg, bg, lam, init, l):
    n = xl.shape[0]
    tm = (2048, 1024, 2048, 512)[l]
    n_tiles = n // tm

    def tile(d, i):
        return i + d * (n_tiles - 1 - 2 * i)

    prev_blocks = tm // LRU_HALO_PREV
    next_blocks = tm // LRU_HALO_NEXT
    layer_dir = lambda shape: pl.BlockSpec((1, 1) + shape, lambda d, i: (l, d) + (0,) * len(shape))
    per_dir = lambda shape: pl.BlockSpec((1,) + shape, lambda d, i: (d,) + (0,) * len(shape))
    return pl.pallas_call(
        functools.partial(_lru_kernel, tm=tm, n_tiles=n_tiles),
        grid=(2, n_tiles),
        in_specs=[pl.BlockSpec((LRU_HALO_PREV, D_LRU),
                               lambda d, i: (jnp.maximum(tile(d, i) * prev_blocks - 1, 0), 0)),
                  pl.BlockSpec((tm, D_LRU), lambda d, i: (tile(d, i), 0)),
                  pl.BlockSpec((LRU_HALO_NEXT, D_LRU),
                               lambda d, i: (jnp.minimum((tile(d, i) + 1) * next_blocks, n // LRU_HALO_NEXT - 1), 0)),
                  _layer_spec((4, D_LRU), l), _layer_spec((1, D_LRU), l),
                  layer_dir((D_LRU, 2 * D_LRU)), layer_dir((1, 2 * D_LRU)), layer_dir((1, D_LRU)),
                  per_dir((BATCH, D_LRU))],
        out_specs=[pl.BlockSpec((1, tm, D_LRU), lambda d, i: (d, tile(d, i), 0)),
                   per_dir((BATCH, D_LRU))],
        out_shape=[jax.ShapeDtypeStruct((2, n, D_LRU), BF16),
                   jax.ShapeDtypeStruct((2, BATCH, D_LRU), F32)],
        scratch_shapes=[pltpu.VMEM((LRU_HALO_PREV + tm + LRU_HALO_NEXT, D_LRU), F32),
                        pltpu.VMEM((tm, D_LRU), F32),
                        pltpu.VMEM((tm, D_LRU), F32),
                        pltpu.VMEM((BATCH, D_LRU), F32)],
        compiler_params=_cparams(("arbitrary", "arbitrary")),
        name="lru",
    )(xl, xl, xl, conv_w, conv_b, wg, bg, lam, init)


def _mix_out_kernel(ys_ref, u_ref, hl_ref, gl_ref, xs_ref, d_ref, wglu_ref, bglu_ref, wout_ref,
                    g1_ref, lng_ref, lnb_ref, sh2_ref, sc2_ref, xo_ref, h2_ref, *, rb):
    g1 = g1_ref[0]
    for r in range(xs_ref.shape[0] // rb):
        rs = slice(rb * r, rb * (r + 1))
        y = (ys_ref[rs, :].astype(F32) + d_ref[0] * u_ref[rs, :].astype(F32)).astype(BF16)
        y = _gelu(y)
        glu = jnp.dot(y, wglu_ref[0], preferred_element_type=F32) + bglu_ref[0]
        y = y * _sigmoid(glu.astype(BF16))
        y_lru = (hl_ref[0, rs, :] + hl_ref[1, rs, :]) * _gelu(gl_ref[rs, :])
        out = jnp.dot(y, wout_ref[0, :D_S5, :], preferred_element_type=F32)
        out = out + jnp.dot(y_lru, wout_ref[0, D_S5:, :], preferred_element_type=F32)
        zsum = DEEPNORM_ALPHA * xs_ref[rs, :] + _per_batch(out, lambda o: o * g1[None])
        xs_new = _ln(zsum) * lng_ref[0] + lnb_ref[0]
        xo_ref[rs, :] = xs_new
        h2_ref[rs, :] = _modulate(xs_new, sh2_ref[0], sc2_ref[0]).astype(BF16)


def _mix_out_call(ys, u, hl, gl, xs, mods, s5_d, w_glu, b_glu, w_out, ln_g, ln_b, l, who):
    n = xs.shape[0]
    tm = 1024
    half = pl.BlockSpec((tm, D_S5), lambda i: (i, 0))
    both = pl.BlockSpec((2, tm, D_S5), lambda i: (0, i, 0))
    row = pl.BlockSpec((tm, D_MODEL), lambda i: (i, 0))
    return pl.pallas_call(
        functools.partial(_mix_out_kernel, rb=(ROW_BLOCK, 1024, 512, 128)[l]),
        grid=(n // tm,),
        in_specs=[half, half, both, half, row,
                  _layer_spec((1, D_S5), l), _layer_spec((D_S5, D_S5), l), _layer_spec((1, D_S5), l),
                  _layer_spec((D_MODEL, D_MODEL), l), _mod_spec(l, who, GATE1),
                  _layer_spec((1, D_MODEL), l), _layer_spec((1, D_MODEL), l),
                  _mod_spec(l, who, SHIFT2), _mod_spec(l, who, SCALE2)],
        out_specs=[row, row],
        out_shape=[jax.ShapeDtypeStruct((n, D_MODEL), F32), jax.ShapeDtypeStruct((n, D_MODEL), BF16)],
        compiler_params=_cparams(("parallel",)),
        name="mix_out",
    )(ys, u, hl, gl, xs, s5_d, w_glu, b_glu, w_out, mods, ln_g, ln_b, mods, mods)


def _ffn_up_kernel(h_ref, w_ref, o_ref):
    o_ref[0] = jnp.dot(h_ref[...], w_ref[0, 0], preferred_element_type=F32).astype(BF16)


def _ffn_up_call(h2, w_up, l):
    n = h2.shape[0]
    tm = 1024
    return pl.pallas_call(
        _ffn_up_kernel,
        grid=(2, n // tm),
        in_specs=[pl.BlockSpec((tm, D_MODEL), lambda s, i: (i, 0)),
                  pl.BlockSpec((1, 1, D_MODEL, D_FF), lambda s, i: (l, s, 0, 0))],
        out_specs=pl.BlockSpec((1, tm, D_FF), lambda s, i: (s, i, 0)),
        out_shape=jax.ShapeDtypeStruct((2, n, D_FF), BF16),
        compiler_params=_cparams(("parallel", "parallel")),
        name="ffn_up",
    )(h2, w_up)


GRID_ROW = GRID_W * BATCH
EPILOGUE_ROWS = 256


_GELU_K = 0.7978845608028654


def _ffn_down_slab_kernel(u_ref, v_ref, cw_ref, cb_ref, wd_ref, xs_ref, g2_ref, lng_ref, lnb_ref,
                          xo_ref, win_ref, act_ref, *, n_tiles, on_grid, fold):
    s = pl.program_id(0)
    c = pl.program_id(1)
    tm = GRID_ROW
    win = win_ref.at[c]
    scale = 0.5 if fold else 1.0

    @pl.when(s == 0)
    def _():
        win[0] = jnp.zeros((tm, FF_CHUNK), BF16)

    @pl.when(s > 0)
    def _():
        edge = jnp.zeros((BATCH, LANES), BF16)
        for k in range(FF_CHUNK // LANES):
            ls = slice(LANES * k, LANES * (k + 1))
            cur = win[1, :, ls]
            nxt = jnp.where(s < n_tiles, u_ref[0, :, ls], jnp.zeros((), BF16))
            w = [(scale * cw_ref[0, t:t + 1, ls]).astype(BF16) for t in range(9)]
            if on_grid:
                rows = ((-1, win[0, :, ls], edge, edge), (0, cur, edge, edge), (1, nxt, edge, edge))
            else:
                rows = ((0, cur, win[0, tm - BATCH:, ls], nxt[:BATCH]),)
            conv = (scale * cb_ref[0, :, ls]).astype(BF16)
            for dy, mid, before, after in rows:
                left = jnp.concatenate([before, mid[:tm - BATCH]], axis=0)
                right = jnp.concatenate([mid[BATCH:], after], axis=0)
                t0 = 3 * (dy + 1)
                conv = conv + (w[t0] * left + w[t0 + 1] * mid + w[t0 + 2] * right)
            if fold:
                hv = conv * v_ref[0, :, ls]
                tanh = jnp.tanh(conv * (2.0 * _GELU_K + (8.0 * 0.044715 * _GELU_K) * (conv * conv)))
                act_ref[:, ls] = hv + hv * tanh
            else:
                act_ref[:, ls] = _gelu(conv) * v_ref[0, :, ls]
        part = jnp.dot(act_ref[...], wd_ref[0], preferred_element_type=F32)

        @pl.when(c == 0)
        def _():
            xo_ref[...] = part

        @pl.when(c == 1)
        def _():
            xo_ref[...] += part
            g2 = g2_ref[0]
            for r in range(tm // EPILOGUE_ROWS):
                rs = slice(EPILOGUE_ROWS * r, EPILOGUE_ROWS * (r + 1))
                zsum = DEEPNORM_ALPHA * xs_ref[rs, :] + _per_batch(xo_ref[rs, :], lambda o: o * g2[None])
                xo_ref[rs, :] = _ln(zsum) * lng_ref[0] + lnb_ref[0]

        win[0] = win[1]

    win[1] = u_ref[0]


def _ffn_down_kernel(u_ref, v_ref, cw_ref, cb_ref, wd_ref, xs_ref, g2_ref, lng_ref, lnb_ref,
                     xo_ref, win_ref, act_ref, wb_ref, *, n_tiles, on_grid, conv_rows):
    s = pl.program_id(0)
    c = pl.program_id(1)
    tm = GRID_ROW
    win = win_ref.at[c]

    @pl.when(s == 0)
    def _():
        win[0] = jnp.zeros((tm, FF_CHUNK), BF16)

    def finish_tile(accumulate):
        live = (s < n_tiles).astype(F32)
        for t in range(9):
            tap = cw_ref[0, t:t + 1, :] * live if (on_grid and t >= 6) else cw_ref[0, t:t + 1, :]
            wb_ref[t] = jnp.broadcast_to(tap, (BATCH, FF_CHUNK)).astype(BF16)
        wb_ref[9] = jnp.broadcast_to(cb_ref[0], (BATCH, FF_CHUNK)).astype(BF16)
        edge = jnp.zeros((BATCH, LANES), BF16)
        nb16 = conv_rows // BATCH
        for r in range(tm // conv_rows):
            r0, r1 = conv_rows * r, conv_rows * (r + 1)
            for k in range(FF_CHUNK // LANES):
                ls = slice(LANES * k, LANES * (k + 1))
                if on_grid:
                    slabs = ((-1, lambda lo, hi: win[0, lo:hi, ls], edge, edge),
                             (0, lambda lo, hi: win[1, lo:hi, ls], edge, edge),
                             (1, lambda lo, hi: u_ref[0, lo:hi, ls], edge, edge))
                else:
                    after = jnp.where(s < n_tiles, u_ref[0, :BATCH, ls], jnp.zeros((), BF16))
                    slabs = ((0, lambda lo, hi: win[1, lo:hi, ls], win[0, tm - BATCH:, ls], after),)
                conv = wb_ref[9, :, ls][None]
                for dy, rows, before, after in slabs:
                    mid = rows(r0, r1)
                    left = rows(r0 - BATCH, r1 - BATCH) if r0 else jnp.concatenate([before, rows(0, r1 - BATCH)], axis=0)
                    right = (rows(r0 + BATCH, r1 + BATCH) if r1 < tm
                             else jnp.concatenate([rows(r0 + BATCH, tm), after], axis=0))
                    t0 = 3 * (dy + 1)
                    for t, x in ((t0, left), (t0 + 1, mid), (t0 + 2, right)):
                        conv = conv + wb_ref[t, :, ls][None] * x.reshape(nb16, BATCH, LANES)
                gate = v_ref[0, r0:r1, ls].reshape(nb16, BATCH, LANES)
                act_ref[r0:r1, ls] = (_gelu(conv) * gate).reshape(conv_rows, LANES)
            part = jnp.dot(act_ref[r0:r1, :], wd_ref[0], preferred_element_type=F32)
            if accumulate:
                xo_ref[r0:r1, :] += part
            else:
                xo_ref[r0:r1, :] = part
        win[0] = win[1]

    @pl.when((s > 0) & (c == 0))
    def _():
        finish_tile(False)

    @pl.when((s > 0) & (c == 1))
    def _():
        finish_tile(True)
        g2 = g2_ref[0]
        for r in range(tm // EPILOGUE_ROWS):
            rs = slice(EPILOGUE_ROWS * r, EPILOGUE_ROWS * (r + 1))
            zsum = DEEPNORM_ALPHA * xs_ref[rs, :] + _per_batch(xo_ref[rs, :], lambda o: o * g2[None])
            xo_ref[rs, :] = _ln(zsum) * lng_ref[0] + lnb_ref[0]

    win[1] = u_ref[0]


def _ffn_down_call(uv, xs, mods, conv_w, conv_b, w_down, ln_g, ln_b, l, who, on_grid, variant):
    n = xs.shape[0]
    tm = GRID_ROW
    n_tiles = n // tm
    n_chunks = D_FF // FF_CHUNK
    assert n_chunks == 2
    done = lambda s: jnp.maximum(s - 1, 0)
    row = pl.BlockSpec((tm, D_MODEL), lambda s, c: (done(s), 0))
    scratch = [pltpu.VMEM((n_chunks, 2, tm, FF_CHUNK), BF16), pltpu.VMEM((tm, FF_CHUNK), BF16)]
    if variant[0] == "slab":
        body = functools.partial(_ffn_down_slab_kernel, n_tiles=n_tiles, on_grid=on_grid, fold=variant[1])
    else:
        body = functools.partial(_ffn_down_kernel, n_tiles=n_tiles, on_grid=on_grid, conv_rows=variant[1])
        scratch.append(pltpu.VMEM((10, BATCH, FF_CHUNK), BF16))
    return pl.pallas_call(
        body,
        grid=(n_tiles + 1, n_chunks),
        in_specs=[pl.BlockSpec((1, tm, FF_CHUNK), lambda s, c: (0, jnp.minimum(s, n_tiles - 1), c)),
                  pl.BlockSpec((1, tm, FF_CHUNK), lambda s, c: (1, done(s), c)),
                  pl.BlockSpec((1, 9, FF_CHUNK), lambda s, c: (l, 0, c)),
                  pl.BlockSpec((1, 1, FF_CHUNK), lambda s, c: (l, 0, c)),
                  pl.BlockSpec((1, FF_CHUNK, D_MODEL), lambda s, c: (l, c, 0)),
                  row, _mod_spec(l, who, GATE2), _layer_spec((1, D_MODEL), l), _layer_spec((1, D_MODEL), l)],
        out_specs=row,
        out_shape=jax.ShapeDtypeStruct((n, D_MODEL), F32),
        scratch_shapes=scratch,
        compiler_params=_cparams(("arbitrary", "arbitrary")),
        name="ffn_down_grid" if on_grid else "ffn_down_seq",
    )(uv, uv, conv_w, conv_b, w_down, xs, mods, ln_g, ln_b)


def _time_major(x):
    b, length, d = x.shape
    return x.transpose(1, 0, 2).reshape(length * b, d)


def kernel(x, c, ctx, c_ctx, w_mod, b_mod, w_in, s5_lam_re, s5_lam_im, s5_log_dt, s5_b_re, s5_b_im, s5_c_re, s5_c_im, s5_d, s5_w_glu, s5_b_glu, lru_conv_w, lru_conv_b, lru_w_a, lru_b_a, lru_w_x, lru_b_x, lru_lam, w_out, ln1_g, ln1_b, ffn_w_up, ffn_conv_w, ffn_conv_b, ffn_w_down, ln2_g, ln2_b):
    bsz, seq, _ = x.shape
    xs = _time_major(x)
    cs = _time_major(ctx)

    c_all = jnp.concatenate([c, jnp.broadcast_to(c_ctx, (BATCH, D_MODEL))], axis=0)
    mods = _mod_call(c_all, w_mod, b_mod)

    row = lambda v: v.reshape(DEPTH, 1, v.shape[-1])
    w_in_b = w_in.astype(BF16)
    mats = jax.vmap(_s5_matrices)(s5_lam_re, s5_lam_im, s5_log_dt, s5_b_re, s5_b_im, s5_c_re, s5_c_im)
    eye_h = jnp.eye(LRU_HEADS, dtype=F32)
    dense = lambda wh: jnp.einsum("ldhij,hk->ldhikj", wh, eye_h).reshape(DEPTH, 2, D_LRU, D_LRU)
    wg = (0.5 * jnp.concatenate([dense(lru_w_a), dense(lru_w_x)], axis=-1)).astype(BF16)
    bg = 0.5 * jnp.concatenate([lru_b_a, lru_b_x], axis=-1).reshape(DEPTH, 2, 1, 2 * D_LRU)
    lam = lru_lam.reshape(DEPTH, 2, 1, D_LRU)
    w_glu = s5_w_glu.astype(BF16)
    w_out_b = w_out.astype(BF16)
    w_up = ffn_w_up.reshape(DEPTH, D_MODEL, 2, D_FF).transpose(0, 2, 1, 3).astype(BF16)
    f_conv_w = ffn_conv_w.reshape(DEPTH, 9, D_FF)
    w_down = ffn_w_down.astype(BF16)
    ln1 = (row(ln1_g), row(ln1_b))
    ln2 = (row(ln2_g), row(ln2_b))

    s5_zero = jnp.zeros((2, S5_LANE_BLOCKS, BATCH, S5_STATE_COLS), F32)
    lru_zero = jnp.zeros((2, BATCH, D_LRU), F32)

    for l in range(DEPTH):
        last = l == DEPTH - 1

        def mixers(stream, who, s5_init, lru_init):
            u, xl, gl = _mix_in_call(stream, mods, w_in_b, l, who)
            ys, s5_fin = _s5_call(u, mats, s5_init, l)
            hl, lru_fin = _lru_call(xl, lru_conv_w, row(lru_conv_b), wg, bg, lam, lru_init, l)
            return (ys, u, hl, gl), (s5_fin, lru_fin)

        def finish(stream, mixed, who, on_grid):
            xs_mid, h2 = _mix_out_call(*mixed, stream, mods, row(s5_d), w_glu, row(s5_b_glu), w_out_b, *ln1, l, who)
            uv = _ffn_up_call(h2, w_up, l)
            variant = (("slab", False), ("slab", True), ("rows", 1024), ("rows", 256))[l]
            return _ffn_down_call(uv, xs_mid, mods, f_conv_w, row(ffn_conv_b), w_down, *ln2, l, who, on_grid,
                                  variant)

        mixed_c, (s5_fin, lru_fin) = mixers(cs, 1, s5_zero, lru_zero)
        mixed_x, _ = mixers(xs, 0, s5_fin, lru_fin)
        xs = finish(xs, mixed_x, 0, True)
        if not last:
            cs = finish(cs, mixed_c, 1, False)

    return xs.reshape(seq, bsz, D_MODEL).transpose(1, 0, 2)
```

```python
import functools

import jax
import jax.numpy as jnp
from jax import lax
from jax.experimental import pallas as pl
from jax.experimental.pallas import tpu as pltpu

F32 = jnp.float32
BF16 = jnp.bfloat16

D_MODEL = 1024
BATCH = 16
DEPTH = 4
GRID_W = 64
D_S5 = 512
D_LRU = 512
S5_GROUP_CH = 16
S5_GROUPS = 32
S5_STATE = 64
LRU_HEADS = 8
LRU_HEAD_DIM = 64
LRU_C = 8.0
D_FF = 2816
N_MOD = 6
LN_EPS = 1e-6
DEEPNORM_ALPHA = (2 * DEPTH) ** 0.25

LANES = 128
GROUPS_PER_LANE_BLOCK = LANES // S5_GROUP_CH
S5_LANE_BLOCKS = D_S5 // LANES
S5_STATE_COLS = 2 * GROUPS_PER_LANE_BLOCK * S5_STATE
S5_CHUNK = 8
S5_TK = S5_CHUNK * LANES
MOD_ROWS = 2 * BATCH
ROW_BLOCK = 256
FF_CHUNK = 1408
VMEM_LIMIT = 60 * 1024 * 1024
SHIFT1, SCALE1, GATE1, SHIFT2, SCALE2, GATE2 = range(N_MOD)


def _cparams(sem):
    return pltpu.CompilerParams(dimension_semantics=sem, vmem_limit_bytes=VMEM_LIMIT)


def _layer_spec(shape, l):
    return pl.BlockSpec((1,) + shape, lambda *_: (l,) + (0,) * len(shape))


def _mod_spec(l, who, slot):
    return pl.BlockSpec((1, BATCH, D_MODEL), lambda *_: (l, who, slot))


def _sigmoid(x):
    return 0.5 * jnp.tanh(0.5 * x) + 0.5


def _gelu(x):
    return 0.5 * x * (1.0 + jnp.tanh(0.7978845608028654 * (x + 0.044715 * (x * x * x))))


def _ln(x):
    mu = jnp.mean(x, axis=-1, keepdims=True)
    xc = x - mu
    var = jnp.mean(xc * xc, axis=-1, keepdims=True)
    return xc * lax.rsqrt(var + LN_EPS)


def _per_batch(x, fn):
    rows, d = x.shape
    return fn(x.reshape(rows // BATCH, BATCH, d)).reshape(rows, d)


def _modulate(x, shift, scale):
    return _per_batch(_ln(x), lambda h: h * (1.0 + scale)[None] + shift[None])


def _mod_kernel(c_ref, w_ref, b_ref, o_ref):
    c = c_ref[...]
    s = c * _sigmoid(c)
    o_ref[0] = jnp.dot(s, w_ref[0], preferred_element_type=F32) + b_ref[0]


def _mod_call(c_all, w_mod, b_mod):
    tn = 1536
    return pl.pallas_call(
        _mod_kernel,
        grid=(DEPTH, N_MOD * D_MODEL // tn),
        in_specs=[pl.BlockSpec((MOD_ROWS, D_MODEL), lambda l, j: (0, 0)),
                  pl.BlockSpec((1, D_MODEL, tn), lambda l, j: (l, 0, j)),
                  pl.BlockSpec((1, 1, tn), lambda l, j: (l, 0, j))],
        out_specs=pl.BlockSpec((1, MOD_ROWS, tn), lambda l, j: (l, 0, j)),
        out_shape=jax.ShapeDtypeStruct((DEPTH, MOD_ROWS, N_MOD * D_MODEL), F32),
        compiler_params=_cparams(("parallel", "parallel")),
        name="mod",
    )(c_all, w_mod, b_mod.reshape(DEPTH, 1, N_MOD * D_MODEL))


def _mix_in_kernel(x_ref, sh_ref, sc_ref, w_ref, u_ref, xl_ref, gl_ref, *xs_out):
    rows = u_ref.shape[0]
    steps = ROW_BLOCK // BATCH
    for r in range(rows // ROW_BLOCK):
        rs = slice(ROW_BLOCK * r, ROW_BLOCK * (r + 1))
        if xs_out:
            x = jnp.concatenate([x_ref[:, steps * r + t, :] for t in range(steps)], axis=0)
            xs_out[0][rs, :] = x
        else:
            x = x_ref[rs, :]
        h = _modulate(x, sh_ref[0], sc_ref[0]).astype(BF16)
        p = jnp.dot(h, w_ref[0], preferred_element_type=F32)
        u_ref[rs, :] = p[:, :D_S5].astype(BF16)
        xl_ref[rs, :] = p[:, D_S5:D_S5 + D_LRU].astype(BF16)
        gl_ref[rs, :] = p[:, D_S5 + D_LRU:].astype(BF16)


def _mix_in_call(xs, mods, w_in, l, who, batch_major=False):
    n = xs.shape[0] * xs.shape[1] if batch_major else xs.shape[0]
    tm = 1024
    out = pl.BlockSpec((tm, D_S5), lambda i: (i, 0))
    out_specs = [out, out, out]
    out_shape = [jax.ShapeDtypeStruct((n, D_S5), BF16)] * 3
    x_spec = pl.BlockSpec((tm, D_MODEL), lambda i: (i, 0))
    if batch_major:
        x_spec = pl.BlockSpec((BATCH, tm // BATCH, D_MODEL), lambda i: (0, i, 0))
        out_specs.append(pl.BlockSpec((tm, D_MODEL), lambda i: (i, 0)))
        out_shape.append(jax.ShapeDtypeStruct((n, D_MODEL), F32))
    return pl.pallas_call(
        _mix_in_kernel,
        grid=(n // tm,),
        in_specs=[x_spec, _mod_spec(l, who, SHIFT1), _mod_spec(l, who, SCALE1),
                  _layer_spec((D_MODEL, D_S5 + 2 * D_LRU), l)],
        out_specs=out_specs,
        out_shape=out_shape,
        compiler_params=_cparams(("parallel",)),
        name="mix_in",
    )(xs, mods, mods, w_in)


def _s5_matrices(lam_re, lam_im, log_dt, b_re, b_im, c_re, c_im):
    T = S5_CHUNK
    lam = lax.complex(lam_re, lam_im)
    lam_dt = lam * jnp.exp(log_dt)[..., None]
    lam_bar = jnp.exp(lam_dt)
    b_bar = ((lam_bar - 1.0) / lam)[..., None] * lax.complex(b_re, b_im)
    c_mat = lax.complex(c_re, c_im)
    steps = jnp.arange(T + 1, dtype=F32)
    pw = jnp.exp(steps[:, None, None, None] * lam_dt[None])
    kd = jnp.real(jnp.einsum("dgop,ndgp,dgpi->ndgio", c_mat, pw[:T], b_bar))
    nb, gpb = S5_LANE_BLOCKS, GROUPS_PER_LANE_BLOCK
    half = gpb * S5_STATE

    lane_group = jnp.arange(LANES) // S5_GROUP_CH
    state_group = jnp.arange(half) // S5_STATE
    rep_ch = jnp.tile(jnp.eye(S5_GROUP_CH, dtype=BF16), (1, gpb))
    rep_st = jnp.tile(jnp.eye(S5_STATE, dtype=BF16), (1, gpb))
    same_ll = (lane_group[:, None] == lane_group[None, :]).astype(BF16)
    same_ls = (lane_group[:, None] == state_group[None, :]).astype(BF16)
    same_sl = same_ls.T

    def spread(x, rep, same, out):
        return jnp.einsum(out, x.astype(BF16), rep, preferred_element_type=BF16) * same

    ar = jnp.arange(T)
    kd = kd.at[0, 0].add(kd[0, 1])
    kb = spread(kd.reshape(T, 2, nb, LANES, S5_GROUP_CH), rep_ch, same_ll, "ndjxo,oq->dnjxq")
    m = jnp.stack([jnp.concatenate([kb[0, b - a] if b >= a else kb[1, a - b] for b in range(T)], axis=-1)
                   for a in range(T)], axis=1).reshape(nb, S5_TK, S5_TK)

    e_in = jnp.stack([T - 1 - ar, ar])
    pw_in = jnp.stack([pw[e_in[d], d] for d in range(2)])
    wc = (pw_in[:, :, :, None, :] * jnp.swapaxes(b_bar, -1, -2)[:, None]).reshape(2, T, nb, LANES, S5_STATE)
    w = jnp.concatenate([spread(part, rep_st, same_ls, "dtjxp,pq->djtxq")
                         for part in (jnp.real(wc), jnp.imag(wc))], axis=-1)
    w = w.reshape(2, nb, S5_TK, S5_STATE_COLS)

    f_out = jnp.stack([ar + 1, T - ar])
    pw_out = jnp.stack([pw[f_out[d], d] for d in range(2)])
    zc = (jnp.swapaxes(c_mat, -1, -2)[:, None] * pw_out[..., None]).reshape(2, T, nb, half, S5_GROUP_CH)
    z = jnp.concatenate([
        jnp.concatenate([spread(part[:, t], rep_ch, same_sl, "djyo,oq->djyq") for t in range(T)], axis=-1)
        for part in (jnp.real(zc), -jnp.imag(zc))], axis=-2)

    a_t = pw[T].reshape(2, nb, half)
    a = jnp.stack([jnp.real(a_t), jnp.imag(a_t)], axis=2)
    return w, m, z, a


def _s5_kernel(u_ref, w_ref, m_ref, z_ref, a_ref, init_ref, y_ref, fin_ref, s_ref, v_ref, spf_ref, spb_ref, *,
               kc, n_tiles):
    ph = pl.program_id(1)
    rt = pl.program_id(2)
    half = S5_STATE_COLS // 2
    rows = kc * BATCH

    @pl.when((ph == 0) & (rt == 0))
    def _():
        s_ref[...] = init_ref[:, 0]

    ucat = jnp.concatenate(
        [u_ref[:, BATCH * t:BATCH * (t + 1), :].reshape(rows, LANES) for t in range(S5_CHUNK)], axis=1)

    def recurrence(d, sp_ref, base):
        v_ref[...] = jnp.dot(ucat, w_ref[0, 0, 0], preferred_element_type=F32)
        a_re = jnp.broadcast_to(a_ref[0, d, 0, 0:1, :], (BATCH, half))
        a_im = jnp.broadcast_to(a_ref[0, d, 0, 1:2, :], (BATCH, half))

        def step(i, carry):
            s_re, s_im = carry
            off = pl.multiple_of((kc - 1 - i if d else i) * BATCH, BATCH)
            dst = pl.multiple_of(base + off, BATCH)
            sp_ref[pl.ds(dst, BATCH), :half] = s_re.astype(BF16)
            sp_ref[pl.ds(dst, BATCH), half:] = s_im.astype(BF16)
            v = v_ref[pl.ds(off, BATCH), :]
            n_re = a_re * s_re - a_im * s_im + v[:, :half]
            n_im = a_re * s_im + a_im * s_re + v[:, half:]
            return n_re, n_im

        s_re, s_im = lax.fori_loop(0, kc, step, (s_ref[d, :, :half], s_ref[d, :, half:]))
        s_ref[d, :, :half] = s_re
        s_ref[d, :, half:] = s_im

        @pl.when(rt == n_tiles - 1)
        def _():
            fin_ref[d, 0] = s_ref[d]

    @pl.when(ph == 0)
    def _():
        recurrence(1, spb_ref, pl.multiple_of((n_tiles - 1 - rt) * rows, rows))

    @pl.when(ph == 1)
    def _():
        recurrence(0, spf_ref, 0)
        s_bwd = spb_ref[pl.ds(pl.multiple_of(rt * rows, rows), rows), :]
        y = (jnp.dot(ucat, m_ref[0, 0], preferred_element_type=F32)
             + jnp.dot(spf_ref[...], z_ref[0, 0, 0], preferred_element_type=F32)
             + jnp.dot(s_bwd, z_ref[0, 1, 0], preferred_element_type=F32))
        for t in range(S5_CHUNK):
            y_ref[:, BATCH * t:BATCH * (t + 1), :] = (
                y[:, LANES * t:LANES * (t + 1)].reshape(kc, BATCH, LANES).astype(BF16))


def _s5_call(u, mats, init, l):
    w, m, z, a = mats
    n = u.shape[0]
    chunk_rows = S5_CHUNK * BATCH
    n_chunks = n // chunk_rows
    kc = min(32, n_chunks)
    n_tiles = n_chunks // kc
    nb = S5_LANE_BLOCKS

    def tile(ph, rt):
        return rt + (1 - ph) * (n_tiles - 1 - 2 * rt)

    both_dirs = lambda shape: pl.BlockSpec((2, 1) + shape, lambda j, ph, rt: (0, j, 0, 0))
    y, fin = pl.pallas_call(
        functools.partial(_s5_kernel, kc=kc, n_tiles=n_tiles),
        grid=(nb, 2, n_tiles),
        in_specs=[pl.BlockSpec((kc, chunk_rows, LANES), lambda j, ph, rt: (tile(ph, rt), 0, j)),
                  pl.BlockSpec((1, 1, 1, S5_TK, S5_STATE_COLS), lambda j, ph, rt: (l, 1 - ph, j, 0, 0)),
                  pl.BlockSpec((1, 1, S5_TK, S5_TK), lambda j, ph, rt: (l, j, 0, 0)),
                  pl.BlockSpec((1, 2, 1, S5_STATE_COLS, S5_TK), lambda j, ph, rt: (l, 0, j, 0, 0)),
                  pl.BlockSpec((1, 2, 1, 2, S5_STATE_COLS // 2), lambda j, ph, rt: (l, 0, j, 0, 0)),
                  both_dirs((BATCH, S5_STATE_COLS))],
        out_specs=[pl.BlockSpec((kc, chunk_rows, LANES), lambda j, ph, rt: (ph * rt, 0, j)),
                   both_dirs((BATCH, S5_STATE_COLS))],
        out_shape=[jax.ShapeDtypeStruct((n_chunks, chunk_rows, D_S5), BF16),
                   jax.ShapeDtypeStruct((2, nb, BATCH, S5_STATE_COLS), F32)],
        scratch_shapes=[pltpu.VMEM((2, BATCH, S5_STATE_COLS), F32),
                        pltpu.VMEM((kc * BATCH, S5_STATE_COLS), F32),
                        pltpu.VMEM((kc * BATCH, S5_STATE_COLS), BF16),
                        pltpu.VMEM((n_chunks * BATCH, S5_STATE_COLS), BF16)],
        compiler_params=_cparams(("arbitrary", "arbitrary", "arbitrary")),
        name="s5",
    )(u.reshape(n_chunks, chunk_rows, D_S5), w, m, z, a, init)
    return y.reshape(n, D_S5), fin


LRU_HALO_PREV = 2 * BATCH
LRU_HALO_NEXT = BATCH


def _lru_kernel(xp_ref, xc_ref, xn_ref, cw_ref, cb_ref, wg_ref, bg_ref, lam_ref, init_ref,
                h_ref, fin_ref, xpad_ref, a_ref, b_ref, hst_ref, *, tm, n_tiles):
    d = pl.program_id(0)
    i = pl.program_id(1)
    te = i + d * (n_tiles - 1 - 2 * i)

    @pl.when(i == 0)
    def _():
        hst_ref[...] = init_ref[0]

    xpad_ref[0:LRU_HALO_PREV, :] = jnp.where(te > 0, xp_ref[...].astype(F32), 0.0)
    xpad_ref[LRU_HALO_PREV:LRU_HALO_PREV + tm, :] = xc_ref[...].astype(F32)
    xpad_ref[LRU_HALO_PREV + tm:, :] = jnp.where(te < n_tiles - 1, xn_ref[...].astype(F32), 0.0)

    xc = cb_ref[0]
    for k in range(4):
        xc = xc + cw_ref[0, k:k + 1, :] * xpad_ref[BATCH * k:BATCH * k + tm, :]

    gates = jnp.tanh(jnp.dot(xc.astype(BF16), wg_ref[0, 0], preferred_element_type=F32) + bg_ref[0, 0])
    z = -lam_ref[0, 0]
    rate = (-0.5 * LRU_C) * (jnp.maximum(z, 0.0) + jnp.log1p(jnp.exp(-jnp.abs(z))))
    log_a = rate * gates[:, :D_LRU] + rate
    a = jnp.exp(log_a)
    half_x = 0.5 * xc
    gated_x = half_x * gates[:, D_LRU:] + half_x
    a_ref[...] = a
    b_ref[...] = jnp.sqrt(-jnp.tanh(log_a) * (a * a + 1.0)) * gated_x

    steps = tm // BATCH

    def step(j, h):
        k = j + d * (steps - 1 - 2 * j)
        off = pl.multiple_of(k * BATCH, BATCH)
        h = a_ref[pl.ds(off, BATCH), :] * h + b_ref[pl.ds(off, BATCH), :]
        b_ref[pl.ds(off, BATCH), :] = h
        return h

    h = lax.fori_loop(0, steps, step, hst_ref[...])
    hst_ref[...] = h
    h_ref[0] = b_ref[...].astype(BF16)

    @pl.when(i == n_tiles - 1)
    def _():
        fin_ref[0] = h


def _lru_call(xl, conv_w, conv_b, wg, bg, lam, init, l):
    n = xl.shape[0]
    tm = 2048
    n_tiles = n // tm

    def tile(d, i):
        return i + d * (n_tiles - 1 - 2 * i)

    prev_blocks = tm // LRU_HALO_PREV
    next_blocks = tm // LRU_HALO_NEXT
    layer_dir = lambda shape: pl.BlockSpec((1, 1) + shape, lambda d, i: (l, d) + (0,) * len(shape))
    per_dir = lambda shape: pl.BlockSpec((1,) + shape, lambda d, i: (d,) + (0,) * len(shape))
    return pl.pallas_call(
        functools.partial(_lru_kernel, tm=tm, n_tiles=n_tiles),
        grid=(2, n_tiles),
        in_specs=[pl.BlockSpec((LRU_HALO_PREV, D_LRU),
                               lambda d, i: (jnp.maximum(tile(d, i) * prev_blocks - 1, 0), 0)),
                  pl.BlockSpec((tm, D_LRU), lambda d, i: (tile(d, i), 0)),
                  pl.BlockSpec((LRU_HALO_NEXT, D_LRU),
                               lambda d, i: (jnp.minimum((tile(d, i) + 1) * next_blocks, n // LRU_HALO_NEXT - 1), 0)),
                  _layer_spec((4, D_LRU), l), _layer_spec((1, D_LRU), l),
                  layer_dir((D_LRU, 2 * D_LRU)), layer_dir((1, 2 * D_LRU)), layer_dir((1, D_LRU)),
                  per_dir((BATCH, D_LRU))],
        out_specs=[pl.BlockSpec((1, tm, D_LRU), lambda d, i: (d, tile(d, i), 0)),
                   per_dir((BATCH, D_LRU))],
        out_shape=[jax.ShapeDtypeStruct((2, n, D_LRU), BF16),
                   jax.ShapeDtypeStruct((2, BATCH, D_LRU), F32)],
        scratch_shapes=[pltpu.VMEM((LRU_HALO_PREV + tm + LRU_HALO_NEXT, D_LRU), F32),
                        pltpu.VMEM((tm, D_LRU), F32),
                        pltpu.VMEM((tm, D_LRU), F32),
                        pltpu.VMEM((BATCH, D_LRU), F32)],
        compiler_params=_cparams(("arbitrary", "arbitrary")),
        name="lru",
    )(xl, xl, xl, conv_w, conv_b, wg, bg, lam, init)


def _mix_out_kernel(ys_ref, u_ref, hl_ref, gl_ref, xs_ref, d_ref, wglu_ref, bglu_ref, wout_ref,
                    g1_ref, lng_ref, lnb_ref, sh2_ref, sc2_ref, xo_ref, h2_ref, *, rb):
    g1 = g1_ref[0]
    for r in range(xs_ref.shape[0] // rb):
        rs = slice(rb * r, rb * (r + 1))
        y = (ys_ref[rs, :].astype(F32) + d_ref[0] * u_ref[rs, :].astype(F32)).astype(BF16)
        y = _gelu(y)
        glu = jnp.dot(y, wglu_ref[0], preferred_element_type=F32) + bglu_ref[0]
        y = y * _sigmoid(glu.astype(BF16))
        y_lru = (hl_ref[0, rs, :] + hl_ref[1, rs, :]) * _gelu(gl_ref[rs, :])
        out = jnp.dot(y, wout_ref[0, :D_S5, :], preferred_element_type=F32)
        out = out + jnp.dot(y_lru, wout_ref[0, D_S5:, :], preferred_element_type=F32)
        zsum = DEEPNORM_ALPHA * xs_ref[rs, :] + _per_batch(out, lambda o: o * g1[None])
        xs_new = _ln(zsum) * lng_ref[0] + lnb_ref[0]
        xo_ref[rs, :] = xs_new
        h2_ref[rs, :] = _modulate(xs_new, sh2_ref[0], sc2_ref[0]).astype(BF16)


def _mix_out_call(ys, u, hl, gl, xs, mods, s5_d, w_glu, b_glu, w_out, ln_g, ln_b, l, who):
    n = xs.shape[0]
    tm = 1024
    half = pl.BlockSpec((tm, D_S5), lambda i: (i, 0))
    both = pl.BlockSpec((2, tm, D_S5), lambda i: (0, i, 0))
    row = pl.BlockSpec((tm, D_MODEL), lambda i: (i, 0))
    return pl.pallas_call(
        functools.partial(_mix_out_kernel, rb=ROW_BLOCK),
        grid=(n // tm,),
        in_specs=[half, half, both, half, row,
                  _layer_spec((1, D_S5), l), _layer_spec((D_S5, D_S5), l), _layer_spec((1, D_S5), l),
                  _layer_spec((D_MODEL, D_MODEL), l), _mod_spec(l, who, GATE1),
                  _layer_spec((1, D_MODEL), l), _layer_spec((1, D_MODEL), l),
                  _mod_spec(l, who, SHIFT2), _mod_spec(l, who, SCALE2)],
        out_specs=[row, row],
        out_shape=[jax.ShapeDtypeStruct((n, D_MODEL), F32), jax.ShapeDtypeStruct((n, D_MODEL), BF16)],
        compiler_params=_cparams(("parallel",)),
        name="mix_out",
    )(ys, u, hl, gl, xs, s5_d, w_glu, b_glu, w_out, mods, ln_g, ln_b, mods, mods)


def _ffn_up_kernel(h_ref, w_ref, o_ref):
    o_ref[0] = jnp.dot(h_ref[...], w_ref[0, 0], preferred_element_type=F32).astype(BF16)


def _ffn_up_call(h2, w_up, l):
    n = h2.shape[0]
    tm = 1024
    return pl.pallas_call(
        _ffn_up_kernel,
        grid=(2, n // tm),
        in_specs=[pl.BlockSpec((tm, D_MODEL), lambda s, i: (i, 0)),
                  pl.BlockSpec((1, 1, D_MODEL, D_FF), lambda s, i: (l, s, 0, 0))],
        out_specs=pl.BlockSpec((1, tm, D_FF), lambda s, i: (s, i, 0)),
        out_shape=jax.ShapeDtypeStruct((2, n, D_FF), BF16),
        compiler_params=_cparams(("parallel", "parallel")),
        name="ffn_up",
    )(h2, w_up)


GRID_ROW = GRID_W * BATCH
EPILOGUE_ROWS = 256


_GELU_K = 0.7978845608028654


def _ffn_down_slab_kernel(u_ref, v_ref, cw_ref, cb_ref, wd_ref, xs_ref, g2_ref, lng_ref, lnb_ref,
                          xo_ref, win_ref, act_ref, *acc, n_tiles, on_grid, fold):
    s = pl.program_id(0)
    c = pl.program_id(1)
    tm = GRID_ROW
    win = win_ref.at[c]
    scale = 0.5 if fold else 1.0

    @pl.when(s == 0)
    def _():
        win[0] = jnp.zeros((tm, FF_CHUNK), BF16)

    @pl.when(s > 0)
    def _():
        edge = jnp.zeros((BATCH, LANES), BF16)
        for k in range(FF_CHUNK // LANES):
            ls = slice(LANES * k, LANES * (k + 1))
            cur = win[1, :, ls]
            nxt = jnp.where(s < n_tiles, u_ref[0, :, ls], jnp.zeros((), BF16))
            w = [(scale * cw_ref[0, t:t + 1, ls]).astype(BF16) for t in range(9)]
            if on_grid:
                rows = ((-1, win[0, :, ls], edge, edge), (0, cur, edge, edge), (1, nxt, edge, edge))
            else:
                rows = ((0, cur, win[0, tm - BATCH:, ls], nxt[:BATCH]),)
            conv = (scale * cb_ref[0, :, ls]).astype(BF16)
            for dy, mid, before, after in rows:
                left = jnp.concatenate([before, mid[:tm - BATCH]], axis=0)
                right = jnp.concatenate([mid[BATCH:], after], axis=0)
                t0 = 3 * (dy + 1)
                conv = conv + (w[t0] * left + w[t0 + 1] * mid + w[t0 + 2] * right)
            if fold:
                act_ref[:, ls] = _half_gelu_gate(conv, v_ref[0, :, ls])
            else:
                act_ref[:, ls] = _gelu(conv) * v_ref[0, :, ls]
        part = jnp.dot(act_ref[...], wd_ref[0], preferred_element_type=F32)
        acc_ref = acc[0] if acc else xo_ref

        @pl.when(c == 0)
        def _():
            acc_ref[...] = part

        @pl.when(c == 1)
        def _():
            acc_ref[...] += part
            _ffn_epilogue(acc_ref, xs_ref, g2_ref, lng_ref, lnb_ref, xo_ref, batch_major=bool(acc))

        win[0] = win[1]

    win[1] = u_ref[0]


def _ffn_epilogue(acc_ref, xs_ref, g2_ref, lng_ref, lnb_ref, xo_ref, batch_major):
    g2 = g2_ref[0]
    steps = EPILOGUE_ROWS // BATCH
    for r in range(GRID_ROW // EPILOGUE_ROWS):
        rs = slice(EPILOGUE_ROWS * r, EPILOGUE_ROWS * (r + 1))
        zsum = DEEPNORM_ALPHA * xs_ref[rs, :] + _per_batch(acc_ref[rs, :], lambda o: o * g2[None])
        res = _ln(zsum) * lng_ref[0] + lnb_ref[0]
        if batch_major:
            for t in range(steps):
                xo_ref[:, steps * r + t, :] = res[BATCH * t:BATCH * (t + 1), :]
        else:
            xo_ref[rs, :] = res


def _half_gelu_gate(h, v):
    hv = h * v
    return hv + hv * jnp.tanh(h * (2.0 * _GELU_K + (8.0 * 0.044715 * _GELU_K) * (h * h)))


def _ffn_down_kernel(u_ref, v_ref, cw_ref, cb_ref, wd_ref, xs_ref, g2_ref, lng_ref, lnb_ref,
                     xo_ref, win_ref, act_ref, *, n_tiles, on_grid, conv_rows):
    s = pl.program_id(0)
    c = pl.program_id(1)
    tm = GRID_ROW
    win = win_ref.at[c]

    @pl.when(s == 0)
    def _():
        win[0] = jnp.zeros((tm, FF_CHUNK), BF16)

    def finish_tile(accumulate):
        edge = jnp.zeros((BATCH, LANES), BF16)
        for r in range(tm // conv_rows):
            r0, r1 = conv_rows * r, conv_rows * (r + 1)
            lo, hi = max(r0 - BATCH, 0), min(r1 + BATCH, tm)
            for k in range(FF_CHUNK // LANES):
                ls = slice(LANES * k, LANES * (k + 1))
                w = [(0.5 * cw_ref[0, t:t + 1, ls]).astype(BF16) for t in range(9)]
                nxt = lambda a, b: jnp.where(s < n_tiles, u_ref[0, a:b, ls], jnp.zeros((), BF16))
                if on_grid:
                    slabs = ((-1, win[0, lo:hi, ls], edge, edge), (0, win[1, lo:hi, ls], edge, edge),
                             (1, nxt(lo, hi), edge, edge))
                else:
                    slabs = ((0, win[1, lo:hi, ls], win[0, tm - BATCH:, ls], nxt(0, BATCH)),)
                conv = (0.5 * cb_ref[0, :, ls]).astype(BF16)
                for dy, ext, before, after in slabs:
                    if r0 == 0:
                        ext = jnp.concatenate([before, ext], axis=0)
                    if r1 == tm:
                        ext = jnp.concatenate([ext, after], axis=0)
                    t0 = 3 * (dy + 1)
                    conv = conv + (w[t0] * ext[:conv_rows] + w[t0 + 1] * ext[BATCH:BATCH + conv_rows]
                                   + w[t0 + 2] * ext[2 * BATCH:])
                act_ref[r0:r1, ls] = _half_gelu_gate(conv, v_ref[0, r0:r1, ls])
            part = jnp.dot(act_ref[r0:r1, :], wd_ref[0], preferred_element_type=F32)
            if accumulate:
                xo_ref[r0:r1, :] += part
            else:
                xo_ref[r0:r1, :] = part
        win[0] = win[1]

    @pl.when((s > 0) & (c == 0))
    def _():
        finish_tile(False)

    @pl.when((s > 0) & (c == 1))
    def _():
        finish_tile(True)
        _ffn_epilogue(xo_ref, xs_ref, g2_ref, lng_ref, lnb_ref, xo_ref, batch_major=False)

    win[1] = u_ref[0]


def _ffn_down_call(uv, xs, mods, conv_w, conv_b, w_down, ln_g, ln_b, l, who, on_grid, variant):
    n = xs.shape[0]
    tm = GRID_ROW
    n_tiles = n // tm
    n_chunks = D_FF // FF_CHUNK
    assert n_chunks == 2
    done = lambda s: jnp.maximum(s - 1, 0)
    row = pl.BlockSpec((tm, D_MODEL), lambda s, c: (done(s), 0))
    scratch = [pltpu.VMEM((n_chunks, 2, tm, FF_CHUNK), BF16), pltpu.VMEM((tm, FF_CHUNK), BF16)]
    out_spec, out_shape = row, jax.ShapeDtypeStruct((n, D_MODEL), F32)
    if variant[0] == "slab":
        body = functools.partial(_ffn_down_slab_kernel, n_tiles=n_tiles, on_grid=on_grid, fold=True)
        if variant[1]:
            scratch.append(pltpu.VMEM((tm, D_MODEL), F32))
            out_spec = pl.BlockSpec((BATCH, tm // BATCH, D_MODEL), lambda s, c: (0, done(s), 0))
            out_shape = jax.ShapeDtypeStruct((BATCH, n // BATCH, D_MODEL), F32)
    else:
        body = functools.partial(_ffn_down_kernel, n_tiles=n_tiles, on_grid=on_grid, conv_rows=variant[1])
    return pl.pallas_call(
        body,
        grid=(n_tiles + 1, n_chunks),
        in_specs=[pl.BlockSpec((1, tm, FF_CHUNK), lambda s, c: (0, jnp.minimum(s, n_tiles - 1), c)),
                  pl.BlockSpec((1, tm, FF_CHUNK), lambda s, c: (1, done(s), c)),
                  pl.BlockSpec((1, 9, FF_CHUNK), lambda s, c: (l, 0, c)),
                  pl.BlockSpec((1, 1, FF_CHUNK), lambda s, c: (l, 0, c)),
                  pl.BlockSpec((1, FF_CHUNK, D_MODEL), lambda s, c: (l, c, 0)),
                  row, _mod_spec(l, who, GATE2), _layer_spec((1, D_MODEL), l), _layer_spec((1, D_MODEL), l)],
        out_specs=out_spec,
        out_shape=out_shape,
        scratch_shapes=scratch,
        compiler_params=_cparams(("arbitrary", "arbitrary")),
        name="ffn_down_grid" if on_grid else "ffn_down_seq",
    )(uv, uv, conv_w, conv_b, w_down, xs, mods, ln_g, ln_b)


def _time_major(x):
    b, length, d = x.shape
    return x.transpose(1, 0, 2).reshape(length * b, d)


def kernel(x, c, ctx, c_ctx, w_mod, b_mod, w_in, s5_lam_re, s5_lam_im, s5_log_dt, s5_b_re, s5_b_im, s5_c_re, s5_c_im, s5_d, s5_w_glu, s5_b_glu, lru_conv_w, lru_conv_b, lru_w_a, lru_b_a, lru_w_x, lru_b_x, lru_lam, w_out, ln1_g, ln1_b, ffn_w_up, ffn_conv_w, ffn_conv_b, ffn_w_down, ln2_g, ln2_b):
    xs, cs = x, ctx

    c_all = jnp.concatenate([c, jnp.broadcast_to(c_ctx, (BATCH, D_MODEL))], axis=0)
    mods = _mod_call(c_all, w_mod, b_mod)

    row = lambda v: v.reshape(DEPTH, 1, v.shape[-1])
    w_in_b = w_in.astype(BF16)
    mats = jax.vmap(_s5_matrices)(s5_lam_re, s5_lam_im, s5_log_dt, s5_b_re, s5_b_im, s5_c_re, s5_c_im)
    eye_h = jnp.eye(LRU_HEADS, dtype=F32)
    dense = lambda wh: jnp.einsum("ldhij,hk->ldhikj", wh, eye_h).reshape(DEPTH, 2, D_LRU, D_LRU)
    wg = (0.5 * jnp.concatenate([dense(lru_w_a), dense(lru_w_x)], axis=-1)).astype(BF16)
    bg = 0.5 * jnp.concatenate([lru_b_a, lru_b_x], axis=-1).reshape(DEPTH, 2, 1, 2 * D_LRU)
    lam = lru_lam.reshape(DEPTH, 2, 1, D_LRU)
    w_glu = s5_w_glu.astype(BF16)
    w_out_b = w_out.astype(BF16)
    w_up = ffn_w_up.reshape(DEPTH, D_MODEL, 2, D_FF).transpose(0, 2, 1, 3).astype(BF16)
    f_conv_w = ffn_conv_w.reshape(DEPTH, 9, D_FF)
    w_down = ffn_w_down.astype(BF16)
    ln1 = (row(ln1_g), row(ln1_b))
    ln2 = (row(ln2_g), row(ln2_b))

    s5_zero = jnp.zeros((2, S5_LANE_BLOCKS, BATCH, S5_STATE_COLS), F32)
    lru_zero = jnp.zeros((2, BATCH, D_LRU), F32)

    for l in range(DEPTH):
        last = l == DEPTH - 1

        def mixers(stream, who, s5_init, lru_init):
            u, xl, gl, *tm_copy = _mix_in_call(stream, mods, w_in_b, l, who, batch_major=(l == 0))
            ys, s5_fin = _s5_call(u, mats, s5_init, l)
            hl, lru_fin = _lru_call(xl, lru_conv_w, row(lru_conv_b), wg, bg, lam, lru_init, l)
            return (ys, u, hl, gl), (s5_fin, lru_fin), (tm_copy[0] if tm_copy else stream)

        def finish(stream, mixed, who, on_grid, variant):
            xs_mid, h2 = _mix_out_call(*mixed, stream, mods, row(s5_d), w_glu, row(s5_b_glu), w_out_b, *ln1, l, who)
            uv = _ffn_up_call(h2, w_up, l)
            return _ffn_down_call(uv, xs_mid, mods, f_conv_w, row(ffn_conv_b), w_down, *ln2, l, who, on_grid,
                                  variant)

        mixed_c, (s5_fin, lru_fin), cs = mixers(cs, 1, s5_zero, lru_zero)
        mixed_x, _, xs = mixers(xs, 0, s5_fin, lru_fin)
        xs = finish(xs, mixed_x, 0, True, (("slab", False), ("rows", 512), ("rows", 256), ("slab", True))[l])
        if not last:
            cs = finish(cs, mixed_c, 1, False, ("slab", False))

    return xs
```

```python
import functools

import jax
import jax.numpy as jnp
from jax import lax
from jax.experimental import pallas as pl
from jax.experimental.pallas import tpu as pltpu

F32 = jnp.float32
BF16 = jnp.bfloat16

D_MODEL = 1024
BATCH = 16
DEPTH = 4
GRID_W = 64
D_S5 = 512
D_LRU = 512
S5_GROUP_CH = 16
S5_GROUPS = 32
S5_STATE = 64
LRU_HEADS = 8
LRU_HEAD_DIM = 64
LRU_C = 8.0
D_FF = 2816
N_MOD = 6
LN_EPS = 1e-6
DEEPNORM_ALPHA = (2 * DEPTH) ** 0.25

LANES = 128
GROUPS_PER_LANE_BLOCK = LANES // S5_GROUP_CH
S5_LANE_BLOCKS = D_S5 // LANES
S5_STATE_COLS = 2 * GROUPS_PER_LANE_BLOCK * S5_STATE
S5_CHUNK = 8
S5_TK = S5_CHUNK * LANES
MOD_ROWS = 2 * BATCH
ROW_BLOCK = 256
FF_CHUNK = 1408
VMEM_LIMIT = 60 * 1024 * 1024
SHIFT1, SCALE1, GATE1, SHIFT2, SCALE2, GATE2 = range(N_MOD)


def _cparams(sem):
    return pltpu.CompilerParams(dimension_semantics=sem, vmem_limit_bytes=VMEM_LIMIT)


def _layer_spec(shape, l):
    return pl.BlockSpec((1,) + shape, lambda *_: (l,) + (0,) * len(shape))


def _mod_spec(l, who, slot):
    return pl.BlockSpec((1, BATCH, D_MODEL), lambda *_: (l, who, slot))


def _sigmoid(x):
    return 0.5 * jnp.tanh(0.5 * x) + 0.5


def _gelu(x):
    return 0.5 * x * (1.0 + jnp.tanh(0.7978845608028654 * (x + 0.044715 * (x * x * x))))


def _ln(x):
    mu = jnp.mean(x, axis=-1, keepdims=True)
    xc = x - mu
    var = jnp.mean(xc * xc, axis=-1, keepdims=True)
    return xc * lax.rsqrt(var + LN_EPS)


def _per_batch(x, fn):
    rows, d = x.shape
    return fn(x.reshape(rows // BATCH, BATCH, d)).reshape(rows, d)


def _modulate(x, shift, scale):
    return _per_batch(_ln(x), lambda h: h * (1.0 + scale)[None] + shift[None])


def _mod_kernel(c_ref, w_ref, b_ref, o_ref):
    c = c_ref[...]
    s = c * _sigmoid(c)
    o_ref[0] = jnp.dot(s, w_ref[0], preferred_element_type=F32) + b_ref[0]


def _mod_call(c_all, w_mod, b_mod):
    tn = 1536
    return pl.pallas_call(
        _mod_kernel,
        grid=(DEPTH, N_MOD * D_MODEL // tn),
        in_specs=[pl.BlockSpec((MOD_ROWS, D_MODEL), lambda l, j: (0, 0)),
                  pl.BlockSpec((1, D_MODEL, tn), lambda l, j: (l, 0, j)),
                  pl.BlockSpec((1, 1, tn), lambda l, j: (l, 0, j))],
        out_specs=pl.BlockSpec((1, MOD_ROWS, tn), lambda l, j: (l, 0, j)),
        out_shape=jax.ShapeDtypeStruct((DEPTH, MOD_ROWS, N_MOD * D_MODEL), F32),
        compiler_params=_cparams(("parallel", "parallel")),
        name="mod",
    )(c_all, w_mod, b_mod.reshape(DEPTH, 1, N_MOD * D_MODEL))


def _mix_in_kernel(x_ref, sh_ref, sc_ref, w_ref, u_ref, xl_ref, gl_ref, *xs_out):
    rows = u_ref.shape[0]
    steps = ROW_BLOCK // BATCH
    for r in range(rows // ROW_BLOCK):
        rs = slice(ROW_BLOCK * r, ROW_BLOCK * (r + 1))
        if xs_out:
            x = jnp.concatenate([x_ref[:, steps * r + t, :] for t in range(steps)], axis=0)
            xs_out[0][rs, :] = x
        else:
            x = x_ref[rs, :]
        h = _modulate(x, sh_ref[0], sc_ref[0]).astype(BF16)
        p = jnp.dot(h, w_ref[0], preferred_element_type=F32)
        u_ref[rs, :] = p[:, :D_S5].astype(BF16)
        xl_ref[rs, :] = p[:, D_S5:D_S5 + D_LRU].astype(BF16)
        gl_ref[rs, :] = p[:, D_S5 + D_LRU:].astype(BF16)


def _mix_in_call(xs, mods, w_in, l, who, batch_major=False):
    n = xs.shape[0] * xs.shape[1] if batch_major else xs.shape[0]
    tm = 1024
    out = pl.BlockSpec((tm, D_S5), lambda i: (i, 0))
    out_specs = [out, out, out]
    out_shape = [jax.ShapeDtypeStruct((n, D_S5), BF16)] * 3
    x_spec = pl.BlockSpec((tm, D_MODEL), lambda i: (i, 0))
    if batch_major:
        x_spec = pl.BlockSpec((BATCH, tm // BATCH, D_MODEL), lambda i: (0, i, 0))
        out_specs.append(pl.BlockSpec((tm, D_MODEL), lambda i: (i, 0)))
        out_shape.append(jax.ShapeDtypeStruct((n, D_MODEL), F32))
    return pl.pallas_call(
        _mix_in_kernel,
        grid=(n // tm,),
        in_specs=[x_spec, _mod_spec(l, who, SHIFT1), _mod_spec(l, who, SCALE1),
                  _layer_spec((D_MODEL, D_S5 + 2 * D_LRU), l)],
        out_specs=out_specs,
        out_shape=out_shape,
        compiler_params=_cparams(("parallel",)),
        name="mix_in",
    )(xs, mods, mods, w_in)


def _s5_matrices(lam_re, lam_im, log_dt, b_re, b_im, c_re, c_im):
    T = S5_CHUNK
    lam = lax.complex(lam_re, lam_im)
    lam_dt = lam * jnp.exp(log_dt)[..., None]
    lam_bar = jnp.exp(lam_dt)
    b_bar = ((lam_bar - 1.0) / lam)[..., None] * lax.complex(b_re, b_im)
    c_mat = lax.complex(c_re, c_im)
    steps = jnp.arange(T + 1, dtype=F32)
    pw = jnp.exp(steps[:, None, None, None] * lam_dt[None])
    kd = jnp.real(jnp.einsum("dgop,ndgp,dgpi->ndgio", c_mat, pw[:T], b_bar))
    nb, gpb = S5_LANE_BLOCKS, GROUPS_PER_LANE_BLOCK
    half = gpb * S5_STATE

    lane_group = jnp.arange(LANES) // S5_GROUP_CH
    state_group = jnp.arange(half) // S5_STATE
    rep_ch = jnp.tile(jnp.eye(S5_GROUP_CH, dtype=BF16), (1, gpb))
    rep_st = jnp.tile(jnp.eye(S5_STATE, dtype=BF16), (1, gpb))
    same_ll = (lane_group[:, None] == lane_group[None, :]).astype(BF16)
    same_ls = (lane_group[:, None] == state_group[None, :]).astype(BF16)
    same_sl = same_ls.T

    def spread(x, rep, same, out):
        return jnp.einsum(out, x.astype(BF16), rep, preferred_element_type=BF16) * same

    ar = jnp.arange(T)
    kd = kd.at[0, 0].add(kd[0, 1])
    kb = spread(kd.reshape(T, 2, nb, LANES, S5_GROUP_CH), rep_ch, same_ll, "ndjxo,oq->dnjxq")
    m = jnp.stack([jnp.concatenate([kb[0, b - a] if b >= a else kb[1, a - b] for b in range(T)], axis=-1)
                   for a in range(T)], axis=1).reshape(nb, S5_TK, S5_TK)

    e_in = jnp.stack([T - 1 - ar, ar])
    pw_in = jnp.stack([pw[e_in[d], d] for d in range(2)])
    wc = (pw_in[:, :, :, None, :] * jnp.swapaxes(b_bar, -1, -2)[:, None]).reshape(2, T, nb, LANES, S5_STATE)
    w = jnp.concatenate([spread(part, rep_st, same_ls, "dtjxp,pq->djtxq")
                         for part in (jnp.real(wc), jnp.imag(wc))], axis=-1)
    w = w.reshape(2, nb, S5_TK, S5_STATE_COLS)

    f_out = jnp.stack([ar + 1, T - ar])
    pw_out = jnp.stack([pw[f_out[d], d] for d in range(2)])
    zc = (jnp.swapaxes(c_mat, -1, -2)[:, None] * pw_out[..., None]).reshape(2, T, nb, half, S5_GROUP_CH)
    z = jnp.concatenate([
        jnp.concatenate([spread(part[:, t], rep_ch, same_sl, "djyo,oq->djyq") for t in range(T)], axis=-1)
        for part in (jnp.real(zc), -jnp.imag(zc))], axis=-2)

    a_t = pw[T].reshape(2, nb, half)
    a = jnp.stack([jnp.real(a_t), jnp.imag(a_t)], axis=2)
    return w, m, z, a


def _s5_kernel(u_ref, w_ref, m_ref, z_ref, a_ref, init_ref, y_ref, fin_ref, s_ref, v_ref, spf_ref, spb_ref, *,
               kc, n_tiles):
    ph = pl.program_id(1)
    rt = pl.program_id(2)
    half = S5_STATE_COLS // 2
    rows = kc * BATCH

    @pl.when((ph == 0) & (rt == 0))
    def _():
        s_ref[...] = init_ref[:, 0]

    ucat = jnp.concatenate(
        [u_ref[:, BATCH * t:BATCH * (t + 1), :].reshape(rows, LANES) for t in range(S5_CHUNK)], axis=1)

    def recurrence(d, sp_ref, base):
        v_ref[...] = jnp.dot(ucat, w_ref[0, 0, 0], preferred_element_type=F32)
        a_re = jnp.broadcast_to(a_ref[0, d, 0, 0:1, :], (BATCH, half))
        a_im = jnp.broadcast_to(a_ref[0, d, 0, 1:2, :], (BATCH, half))

        def step(i, carry):
            s_re, s_im = carry
            off = pl.multiple_of((kc - 1 - i if d else i) * BATCH, BATCH)
            dst = pl.multiple_of(base + off, BATCH)
            sp_ref[pl.ds(dst, BATCH), :half] = s_re.astype(BF16)
            sp_ref[pl.ds(dst, BATCH), half:] = s_im.astype(BF16)
            v = v_ref[pl.ds(off, BATCH), :]
            n_re = a_re * s_re - a_im * s_im + v[:, :half]
            n_im = a_re * s_im + a_im * s_re + v[:, half:]
            return n_re, n_im

        s_re, s_im = lax.fori_loop(0, kc, step, (s_ref[d, :, :half], s_ref[d, :, half:]))
        s_ref[d, :, :half] = s_re
        s_ref[d, :, half:] = s_im

        @pl.when(rt == n_tiles - 1)
        def _():
            fin_ref[d, 0] = s_ref[d]

    @pl.when(ph == 0)
    def _():
        recurrence(1, spb_ref, pl.multiple_of((n_tiles - 1 - rt) * rows, rows))

    @pl.when(ph == 1)
    def _():
        recurrence(0, spf_ref, 0)
        s_bwd = spb_ref[pl.ds(pl.multiple_of(rt * rows, rows), rows), :]
        y = (jnp.dot(ucat, m_ref[0, 0], preferred_element_type=F32)
             + jnp.dot(spf_ref[...], z_ref[0, 0, 0], preferred_element_type=F32)
             + jnp.dot(s_bwd, z_ref[0, 1, 0], preferred_element_type=F32))
        for t in range(S5_CHUNK):
            y_ref[:, BATCH * t:BATCH * (t + 1), :] = (
                y[:, LANES * t:LANES * (t + 1)].reshape(kc, BATCH, LANES).astype(BF16))


def _s5_call(u, mats, init, l):
    w, m, z, a = mats
    n = u.shape[0]
    chunk_rows = S5_CHUNK * BATCH
    n_chunks = n // chunk_rows
    kc = min(32, n_chunks)
    n_tiles = n_chunks // kc
    nb = S5_LANE_BLOCKS

    def tile(ph, rt):
        return rt + (1 - ph) * (n_tiles - 1 - 2 * rt)

    both_dirs = lambda shape: pl.BlockSpec((2, 1) + shape, lambda j, ph, rt: (0, j, 0, 0))
    y, fin = pl.pallas_call(
        functools.partial(_s5_kernel, kc=kc, n_tiles=n_tiles),
        grid=(nb, 2, n_tiles),
        in_specs=[pl.BlockSpec((kc, chunk_rows, LANES), lambda j, ph, rt: (tile(ph, rt), 0, j)),
                  pl.BlockSpec((1, 1, 1, S5_TK, S5_STATE_COLS), lambda j, ph, rt: (l, 1 - ph, j, 0, 0)),
                  pl.BlockSpec((1, 1, S5_TK, S5_TK), lambda j, ph, rt: (l, j, 0, 0)),
                  pl.BlockSpec((1, 2, 1, S5_STATE_COLS, S5_TK), lambda j, ph, rt: (l, 0, j, 0, 0)),
                  pl.BlockSpec((1, 2, 1, 2, S5_STATE_COLS // 2), lambda j, ph, rt: (l, 0, j, 0, 0)),
                  both_dirs((BATCH, S5_STATE_COLS))],
        out_specs=[pl.BlockSpec((kc, chunk_rows, LANES), lambda j, ph, rt: (ph * rt, 0, j)),
                   both_dirs((BATCH, S5_STATE_COLS))],
        out_shape=[jax.ShapeDtypeStruct((n_chunks, chunk_rows, D_S5), BF16),
                   jax.ShapeDtypeStruct((2, nb, BATCH, S5_STATE_COLS), F32)],
        scratch_shapes=[pltpu.VMEM((2, BATCH, S5_STATE_COLS), F32),
                        pltpu.VMEM((kc * BATCH, S5_STATE_COLS), F32),
                        pltpu.VMEM((kc * BATCH, S5_STATE_COLS), BF16),
                        pltpu.VMEM((n_chunks * BATCH, S5_STATE_COLS), BF16)],
        compiler_params=_cparams(("arbitrary", "arbitrary", "arbitrary")),
        name="s5",
    )(u.reshape(n_chunks, chunk_rows, D_S5), w, m, z, a, init)
    return y.reshape(n, D_S5), fin


LRU_HALO_PREV = 2 * BATCH
LRU_HALO_NEXT = BATCH


def _lru_kernel(xp_ref, xc_ref, xn_ref, cw_ref, cb_ref, wg_ref, bg_ref, lam_ref, init_ref,
                h_ref, fin_ref, xpad_ref, a_ref, b_ref, hst_ref, *, tm, n_tiles):
    d = pl.program_id(0)
    i = pl.program_id(1)
    te = i + d * (n_tiles - 1 - 2 * i)

    @pl.when(i == 0)
    def _():
        hst_ref[...] = init_ref[0]

    xpad_ref[0:LRU_HALO_PREV, :] = jnp.where(te > 0, xp_ref[...].astype(F32), 0.0)
    xpad_ref[LRU_HALO_PREV:LRU_HALO_PREV + tm, :] = xc_ref[...].astype(F32)
    xpad_ref[LRU_HALO_PREV + tm:, :] = jnp.where(te < n_tiles - 1, xn_ref[...].astype(F32), 0.0)

    xc = cb_ref[0]
    for k in range(4):
        xc = xc + cw_ref[0, k:k + 1, :] * xpad_ref[BATCH * k:BATCH * k + tm, :]

    gates = jnp.tanh(jnp.dot(xc.astype(BF16), wg_ref[0, 0], preferred_element_type=F32) + bg_ref[0, 0])
    z = -lam_ref[0, 0]
    rate = (-0.5 * LRU_C) * (jnp.maximum(z, 0.0) + jnp.log1p(jnp.exp(-jnp.abs(z))))
    log_a = rate * gates[:, :D_LRU] + rate
    a = jnp.exp(log_a)
    half_x = 0.5 * xc
    gated_x = half_x * gates[:, D_LRU:] + half_x
    a_ref[...] = a
    b_ref[...] = jnp.sqrt(-jnp.tanh(log_a) * (a * a + 1.0)) * gated_x

    steps = tm // BATCH

    def step(j, h):
        k = j + d * (steps - 1 - 2 * j)
        off = pl.multiple_of(k * BATCH, BATCH)
        h = a_ref[pl.ds(off, BATCH), :] * h + b_ref[pl.ds(off, BATCH), :]
        b_ref[pl.ds(off, BATCH), :] = h
        return h

    h = lax.fori_loop(0, steps, step, hst_ref[...])
    hst_ref[...] = h
    h_ref[0] = b_ref[...].astype(BF16)

    @pl.when(i == n_tiles - 1)
    def _():
        fin_ref[0] = h


def _lru_call(xl, conv_w, conv_b, wg, bg, lam, init, l):
    n = xl.shape[0]
    tm = 2048
    n_tiles = n // tm

    def tile(d, i):
        return i + d * (n_tiles - 1 - 2 * i)

    prev_blocks = tm // LRU_HALO_PREV
    next_blocks = tm // LRU_HALO_NEXT
    layer_dir = lambda shape: pl.BlockSpec((1, 1) + shape, lambda d, i: (l, d) + (0,) * len(shape))
    per_dir = lambda shape: pl.BlockSpec((1,) + shape, lambda d, i: (d,) + (0,) * len(shape))
    return pl.pallas_call(
        functools.partial(_lru_kernel, tm=tm, n_tiles=n_tiles),
        grid=(2, n_tiles),
        in_specs=[pl.BlockSpec((LRU_HALO_PREV, D_LRU),
                               lambda d, i: (jnp.maximum(tile(d, i) * prev_blocks - 1, 0), 0)),
                  pl.BlockSpec((tm, D_LRU), lambda d, i: (tile(d, i), 0)),
                  pl.BlockSpec((LRU_HALO_NEXT, D_LRU),
                               lambda d, i: (jnp.minimum((tile(d, i) + 1) * next_blocks, n // LRU_HALO_NEXT - 1), 0)),
                  _layer_spec((4, D_LRU), l), _layer_spec((1, D_LRU), l),
                  layer_dir((D_LRU, 2 * D_LRU)), layer_dir((1, 2 * D_LRU)), layer_dir((1, D_LRU)),
                  per_dir((BATCH, D_LRU))],
        out_specs=[pl.BlockSpec((1, tm, D_LRU), lambda d, i: (d, tile(d, i), 0)),
                   per_dir((BATCH, D_LRU))],
        out_shape=[jax.ShapeDtypeStruct((2, n, D_LRU), BF16),
                   jax.ShapeDtypeStruct((2, BATCH, D_LRU), F32)],
        scratch_shapes=[pltpu.VMEM((LRU_HALO_PREV + tm + LRU_HALO_NEXT, D_LRU), F32),
                        pltpu.VMEM((tm, D_LRU), F32),
                        pltpu.VMEM((tm, D_LRU), F32),
                        pltpu.VMEM((BATCH, D_LRU), F32)],
        compiler_params=_cparams(("arbitrary", "arbitrary")),
        name="lru",
    )(xl, xl, xl, conv_w, conv_b, wg, bg, lam, init)


def _mix_out_kernel(ys_ref, u_ref, hl_ref, gl_ref, xs_ref, d_ref, wglu_ref, bglu_ref, wout_ref,
                    g1_ref, lng_ref, lnb_ref, sh2_ref, sc2_ref, xo_ref, h2_ref):
    g1 = g1_ref[0]
    for r in range(xs_ref.shape[0] // ROW_BLOCK):
        rs = slice(ROW_BLOCK * r, ROW_BLOCK * (r + 1))
        y = (ys_ref[rs, :].astype(F32) + d_ref[0] * u_ref[rs, :].astype(F32)).astype(BF16)
        y = _gelu(y)
        glu = jnp.dot(y, wglu_ref[0], preferred_element_type=F32) + bglu_ref[0]
        y = y * _sigmoid(glu.astype(BF16))
        y_lru = (hl_ref[0, rs, :] + hl_ref[1, rs, :]) * _gelu(gl_ref[rs, :])
        out = jnp.dot(y, wout_ref[0, :D_S5, :], preferred_element_type=F32)
        out = out + jnp.dot(y_lru, wout_ref[0, D_S5:, :], preferred_element_type=F32)
        zsum = DEEPNORM_ALPHA * xs_ref[rs, :] + _per_batch(out, lambda o: o * g1[None])
        xs_new = _ln(zsum) * lng_ref[0] + lnb_ref[0]
        xo_ref[rs, :] = xs_new
        h2_ref[rs, :] = _modulate(xs_new, sh2_ref[0], sc2_ref[0]).astype(BF16)


def _mix_out_call(ys, u, hl, gl, xs, mods, s5_d, w_glu, b_glu, w_out, ln_g, ln_b, l, who):
    n = xs.shape[0]
    tm = 1024
    half = pl.BlockSpec((tm, D_S5), lambda i: (i, 0))
    both = pl.BlockSpec((2, tm, D_S5), lambda i: (0, i, 0))
    row = pl.BlockSpec((tm, D_MODEL), lambda i: (i, 0))
    return pl.pallas_call(
        _mix_out_kernel,
        grid=(n // tm,),
        in_specs=[half, half, both, half, row,
                  _layer_spec((1, D_S5), l), _layer_spec((D_S5, D_S5), l), _layer_spec((1, D_S5), l),
                  _layer_spec((D_MODEL, D_MODEL), l), _mod_spec(l, who, GATE1),
                  _layer_spec((1, D_MODEL), l), _layer_spec((1, D_MODEL), l),
                  _mod_spec(l, who, SHIFT2), _mod_spec(l, who, SCALE2)],
        out_specs=[row, row],
        out_shape=[jax.ShapeDtypeStruct((n, D_MODEL), F32), jax.ShapeDtypeStruct((n, D_MODEL), BF16)],
        compiler_params=_cparams(("parallel",)),
        name="mix_out",
    )(ys, u, hl, gl, xs, s5_d, w_glu, b_glu, w_out, mods, ln_g, ln_b, mods, mods)


def _ffn_up_kernel(h_ref, w_ref, o_ref):
    o_ref[0] = jnp.dot(h_ref[...], w_ref[0], preferred_element_type=F32).astype(BF16)


def _ffn_up_call(h2, w_up, l):
    n = h2.shape[0]
    tm = 1024
    return pl.pallas_call(
        _ffn_up_kernel,
        grid=(2, n // tm),
        in_specs=[pl.BlockSpec((tm, D_MODEL), lambda s, i: (i, 0)),
                  pl.BlockSpec((1, D_MODEL, D_FF), lambda s, i: (l, 0, s))],
        out_specs=pl.BlockSpec((1, tm, D_FF), lambda s, i: (s, i, 0)),
        out_shape=jax.ShapeDtypeStruct((2, n, D_FF), BF16),
        compiler_params=_cparams(("parallel", "parallel")),
        name="ffn_up",
    )(h2, w_up)


GRID_ROW = GRID_W * BATCH
EPILOGUE_ROWS = 256
CONV_ROWS = 512
_GELU_K = 0.7978845608028654


def _half_gelu_gate(h, v):
    hv = h * v
    return hv + hv * jnp.tanh(h * (2.0 * _GELU_K + (8.0 * 0.044715 * _GELU_K) * (h * h)))


def _ffn_down_kernel(u_ref, v_ref, cw_ref, cb_ref, wd_ref, xs_ref, g2_ref, lng_ref, lnb_ref,
                     xo_ref, win_ref, act_ref, *acc, n_tiles, on_grid):
    s = pl.program_id(0)
    c = pl.program_id(1)
    tm = GRID_ROW
    win = win_ref.at[c]
    acc_ref = acc[0] if acc else xo_ref

    @pl.when(s == 0)
    def _():
        win[0] = jnp.zeros((tm, FF_CHUNK), BF16)

    def finish_tile(accumulate):
        edge = jnp.zeros((BATCH, LANES), BF16)
        for r in range(tm // CONV_ROWS):
            r0, r1 = CONV_ROWS * r, CONV_ROWS * (r + 1)
            lo, hi = max(r0 - BATCH, 0), min(r1 + BATCH, tm)
            for k in range(FF_CHUNK // LANES):
                ls = slice(LANES * k, LANES * (k + 1))
                w = [(0.5 * cw_ref[0, t:t + 1, ls]).astype(BF16) for t in range(9)]
                nxt = lambda a, b: jnp.where(s < n_tiles, u_ref[0, a:b, ls], jnp.zeros((), BF16))
                if on_grid:
                    slabs = ((-1, win[0, lo:hi, ls], edge, edge), (0, win[1, lo:hi, ls], edge, edge),
                             (1, nxt(lo, hi), edge, edge))
                else:
                    slabs = ((0, win[1, lo:hi, ls], win[0, tm - BATCH:, ls], nxt(0, BATCH)),)
                conv = (0.5 * cb_ref[0, :, ls]).astype(BF16)
                for dy, ext, before, after in slabs:
                    if r0 == 0:
                        ext = jnp.concatenate([before, ext], axis=0)
                    if r1 == tm:
                        ext = jnp.concatenate([ext, after], axis=0)
                    t0 = 3 * (dy + 1)
                    conv = conv + (w[t0] * ext[:CONV_ROWS] + w[t0 + 1] * ext[BATCH:BATCH + CONV_ROWS]
                                   + w[t0 + 2] * ext[2 * BATCH:])
                act_ref[r0:r1, ls] = _half_gelu_gate(conv, v_ref[0, r0:r1, ls])
            part = jnp.dot(act_ref[r0:r1, :], wd_ref[0], preferred_element_type=F32)
            if accumulate:
                acc_ref[r0:r1, :] += part
            else:
                acc_ref[r0:r1, :] = part
        win[0] = win[1]

    @pl.when((s > 0) & (c == 0))
    def _():
        finish_tile(False)

    @pl.when((s > 0) & (c == 1))
    def _():
        finish_tile(True)
        g2 = g2_ref[0]
        steps = EPILOGUE_ROWS // BATCH
        for r in range(tm // EPILOGUE_ROWS):
            rs = slice(EPILOGUE_ROWS * r, EPILOGUE_ROWS * (r + 1))
            zsum = DEEPNORM_ALPHA * xs_ref[rs, :] + _per_batch(acc_ref[rs, :], lambda o: o * g2[None])
            res = _ln(zsum) * lng_ref[0] + lnb_ref[0]
            if acc:
                for t in range(steps):
                    xo_ref[:, steps * r + t, :] = res[BATCH * t:BATCH * (t + 1), :]
            else:
                xo_ref[rs, :] = res

    win[1] = u_ref[0]


def _ffn_down_call(uv, xs, mods, conv_w, conv_b, w_down, ln_g, ln_b, l, who, on_grid, batch_major_out=False):
    n = xs.shape[0]
    tm = GRID_ROW
    n_tiles = n // tm
    n_chunks = D_FF // FF_CHUNK
    assert n_chunks == 2
    done = lambda s: jnp.maximum(s - 1, 0)
    row = pl.BlockSpec((tm, D_MODEL), lambda s, c: (done(s), 0))
    scratch = [pltpu.VMEM((n_chunks, 2, tm, FF_CHUNK), BF16), pltpu.VMEM((tm, FF_CHUNK), BF16)]
    out_spec, out_shape = row, jax.ShapeDtypeStruct((n, D_MODEL), F32)
    if batch_major_out:
        scratch.append(pltpu.VMEM((tm, D_MODEL), F32))
        out_spec = pl.BlockSpec((BATCH, tm // BATCH, D_MODEL), lambda s, c: (0, done(s), 0))
        out_shape = jax.ShapeDtypeStruct((BATCH, n // BATCH, D_MODEL), F32)
    return pl.pallas_call(
        functools.partial(_ffn_down_kernel, n_tiles=n_tiles, on_grid=on_grid),
        grid=(n_tiles + 1, n_chunks),
        in_specs=[pl.BlockSpec((1, tm, FF_CHUNK), lambda s, c: (0, jnp.minimum(s, n_tiles - 1), c)),
                  pl.BlockSpec((1, tm, FF_CHUNK), lambda s, c: (1, done(s), c)),
                  pl.BlockSpec((1, 9, FF_CHUNK), lambda s, c: (l, 0, c)),
                  pl.BlockSpec((1, 1, FF_CHUNK), lambda s, c: (l, 0, c)),
                  pl.BlockSpec((1, FF_CHUNK, D_MODEL), lambda s, c: (l, c, 0)),
                  row, _mod_spec(l, who, GATE2), _layer_spec((1, D_MODEL), l), _layer_spec((1, D_MODEL), l)],
        out_specs=out_spec,
        out_shape=out_shape,
        scratch_shapes=scratch,
        compiler_params=_cparams(("arbitrary", "arbitrary")),
        name="ffn_down_grid" if on_grid else "ffn_down_seq",
    )(uv, uv, conv_w, conv_b, w_down, xs, mods, ln_g, ln_b)


def kernel(x, c, ctx, c_ctx, w_mod, b_mod, w_in, s5_lam_re, s5_lam_im, s5_log_dt, s5_b_re, s5_b_im, s5_c_re, s5_c_im, s5_d, s5_w_glu, s5_b_glu, lru_conv_w, lru_conv_b, lru_w_a, lru_b_a, lru_w_x, lru_b_x, lru_lam, w_out, ln1_g, ln1_b, ffn_w_up, ffn_conv_w, ffn_conv_b, ffn_w_down, ln2_g, ln2_b):
    xs, cs = x, ctx

    c_all = jnp.concatenate([c, jnp.broadcast_to(c_ctx, (BATCH, D_MODEL))], axis=0)
    mods = _mod_call(c_all, w_mod, b_mod)

    row = lambda v: v.reshape(DEPTH, 1, v.shape[-1])
    w_in_b = w_in.astype(BF16)
    mats = jax.vmap(_s5_matrices)(s5_lam_re, s5_lam_im, s5_log_dt, s5_b_re, s5_b_im, s5_c_re, s5_c_im)
    eye_h = jnp.eye(LRU_HEADS, dtype=F32)
    dense = lambda wh: jnp.einsum("ldhij,hk->ldhikj", wh, eye_h).reshape(DEPTH, 2, D_LRU, D_LRU)
    wg = (0.5 * jnp.concatenate([dense(lru_w_a), dense(lru_w_x)], axis=-1)).astype(BF16)
    bg = 0.5 * jnp.concatenate([lru_b_a, lru_b_x], axis=-1).reshape(DEPTH, 2, 1, 2 * D_LRU)
    lam = lru_lam.reshape(DEPTH, 2, 1, D_LRU)
    w_glu = s5_w_glu.astype(BF16)
    w_out_b = w_out.astype(BF16)
    w_up = ffn_w_up.astype(BF16)
    f_conv_w = ffn_conv_w.reshape(DEPTH, 9, D_FF)
    w_down = ffn_w_down.astype(BF16)
    ln1 = (row(ln1_g), row(ln1_b))
    ln2 = (row(ln2_g), row(ln2_b))

    s5_zero = jnp.zeros((2, S5_LANE_BLOCKS, BATCH, S5_STATE_COLS), F32)
    lru_zero = jnp.zeros((2, BATCH, D_LRU), F32)

    for l in range(DEPTH):
        last = l == DEPTH - 1

        def mixers(stream, who, s5_init, lru_init):
            u, xl, gl, *tm_copy = _mix_in_call(stream, mods, w_in_b, l, who, batch_major=(l == 0))
            ys, s5_fin = _s5_call(u, mats, s5_init, l)
            hl, lru_fin = _lru_call(xl, lru_conv_w, row(lru_conv_b), wg, bg, lam, lru_init, l)
            return (ys, u, hl, gl), (s5_fin, lru_fin), (tm_copy[0] if tm_copy else stream)

        def finish(stream, mixed, who, on_grid):
            xs_mid, h2 = _mix_out_call(*mixed, stream, mods, row(s5_d), w_glu, row(s5_b_glu), w_out_b, *ln1, l, who)
            uv = _ffn_up_call(h2, w_up, l)
            return _ffn_down_call(uv, xs_mid, mods, f_conv_w, row(ffn_conv_b), w_down, *ln2, l, who, on_grid,
                                  batch_major_out=last)

        mixed_c, (s5_fin, lru_fin), cs = mixers(cs, 1, s5_zero, lru_zero)
        mixed_x, _, xs = mixers(xs, 0, s5_fin, lru_fin)
        xs = finish(xs, mixed_x, 0, True)
        if not last:
            cs = finish(cs, mixed_c, 1, False)

    return xs
```

```python
import functools

import jax
import jax.numpy as jnp
from jax import lax
from jax.experimental import pallas as pl
from jax.experimental.pallas import tpu as pltpu

F32 = jnp.float32
BF16 = jnp.bfloat16

D_MODEL = 1024
BATCH = 16
DEPTH = 4
GRID_W = 64
D_S5 = 512
D_LRU = 512
S5_GROUP_CH = 16
S5_GROUPS = 32
S5_STATE = 64
LRU_HEADS = 8
LRU_HEAD_DIM = 64
LRU_C = 8.0
D_FF = 2816
N_MOD = 6
LN_EPS = 1e-6
DEEPNORM_ALPHA = (2 * DEPTH) ** 0.25

LANES = 128
GROUPS_PER_LANE_BLOCK = LANES // S5_GROUP_CH
S5_LANE_BLOCKS = D_S5 // LANES
S5_STATE_COLS = 2 * GROUPS_PER_LANE_BLOCK * S5_STATE
S5_CHUNK = 8
S5_TK = S5_CHUNK * LANES
MOD_ROWS = 2 * BATCH
ROW_BLOCK = 256
FF_CHUNK = 1408
VMEM_LIMIT = 60 * 1024 * 1024
SHIFT1, SCALE1, GATE1, SHIFT2, SCALE2, GATE2 = range(N_MOD)


def _cparams(sem):
    return pltpu.CompilerParams(dimension_semantics=sem, vmem_limit_bytes=VMEM_LIMIT)


def _layer_spec(shape, l):
    return pl.BlockSpec((1,) + shape, lambda *_: (l,) + (0,) * len(shape))


def _mod_spec(l, who, slot):
    return pl.BlockSpec((1, BATCH, D_MODEL), lambda *_: (l, who, slot))


def _sigmoid(x):
    return 0.5 * jnp.tanh(0.5 * x) + 0.5


def _gelu(x):
    return 0.5 * x * (1.0 + jnp.tanh(0.7978845608028654 * (x + 0.044715 * (x * x * x))))


def _ln(x):
    mu = jnp.mean(x, axis=-1, keepdims=True)
    xc = x - mu
    var = jnp.mean(xc * xc, axis=-1, keepdims=True)
    return xc * lax.rsqrt(var + LN_EPS)


def _per_batch(x, fn):
    rows, d = x.shape
    return fn(x.reshape(rows // BATCH, BATCH, d)).reshape(rows, d)


def _modulate(x, shift, scale):
    return _per_batch(_ln(x), lambda h: h * (1.0 + scale)[None] + shift[None])


def _mod_kernel(c_ref, w_ref, b_ref, o_ref):
    c = c_ref[...]
    s = c * _sigmoid(c)
    o_ref[0] = jnp.dot(s, w_ref[0], preferred_element_type=F32) + b_ref[0]


def _mod_call(c_all, w_mod, b_mod):
    tn = 1536
    return pl.pallas_call(
        _mod_kernel,
        grid=(DEPTH, N_MOD * D_MODEL // tn),
        in_specs=[pl.BlockSpec((MOD_ROWS, D_MODEL), lambda l, j: (0, 0)),
                  pl.BlockSpec((1, D_MODEL, tn), lambda l, j: (l, 0, j)),
                  pl.BlockSpec((1, 1, tn), lambda l, j: (l, 0, j))],
        out_specs=pl.BlockSpec((1, MOD_ROWS, tn), lambda l, j: (l, 0, j)),
        out_shape=jax.ShapeDtypeStruct((DEPTH, MOD_ROWS, N_MOD * D_MODEL), F32),
        compiler_params=_cparams(("parallel", "parallel")),
        name="mod",
    )(c_all, w_mod, b_mod.reshape(DEPTH, 1, N_MOD * D_MODEL))


def _mix_in_kernel(x_ref, sh_ref, sc_ref, w_ref, u_ref, xl_ref, gl_ref, *xs_out):
    rows = u_ref.shape[0]
    steps = ROW_BLOCK // BATCH
    for r in range(rows // ROW_BLOCK):
        rs = slice(ROW_BLOCK * r, ROW_BLOCK * (r + 1))
        if xs_out:
            x = jnp.concatenate([x_ref[:, steps * r + t, :] for t in range(steps)], axis=0)
            xs_out[0][rs, :] = x
        else:
            x = x_ref[rs, :]
        h = _modulate(x, sh_ref[0], sc_ref[0]).astype(BF16)
        p = jnp.dot(h, w_ref[0], preferred_element_type=F32)
        u_ref[rs, :] = p[:, :D_S5].astype(BF16)
        xl_ref[rs, :] = p[:, D_S5:D_S5 + D_LRU].astype(BF16)
        gl_ref[rs, :] = p[:, D_S5 + D_LRU:].astype(BF16)


def _mix_in_call(xs, mods, w_in, l, who, batch_major=False):
    n = xs.shape[0] * xs.shape[1] if batch_major else xs.shape[0]
    tm = 1024
    out = pl.BlockSpec((tm, D_S5), lambda i: (i, 0))
    out_specs = [out, out, out]
    out_shape = [jax.ShapeDtypeStruct((n, D_S5), BF16)] * 3
    x_spec = pl.BlockSpec((tm, D_MODEL), lambda i: (i, 0))
    if batch_major:
        x_spec = pl.BlockSpec((BATCH, tm // BATCH, D_MODEL), lambda i: (0, i, 0))
        out_specs.append(pl.BlockSpec((tm, D_MODEL), lambda i: (i, 0)))
        out_shape.append(jax.ShapeDtypeStruct((n, D_MODEL), F32))
    return pl.pallas_call(
        _mix_in_kernel,
        grid=(n // tm,),
        in_specs=[x_spec, _mod_spec(l, who, SHIFT1), _mod_spec(l, who, SCALE1),
                  _layer_spec((D_MODEL, D_S5 + 2 * D_LRU), l)],
        out_specs=out_specs,
        out_shape=out_shape,
        compiler_params=_cparams(("parallel",)),
        name="mix_in",
    )(xs, mods, mods, w_in)


def _s5_matrices(lam_re, lam_im, log_dt, b_re, b_im, c_re, c_im):
    T = S5_CHUNK
    lam = lax.complex(lam_re, lam_im)
    lam_dt = lam * jnp.exp(log_dt)[..., None]
    lam_bar = jnp.exp(lam_dt)
    b_bar = ((lam_bar - 1.0) / lam)[..., None] * lax.complex(b_re, b_im)
    c_mat = lax.complex(c_re, c_im)
    steps = jnp.arange(T + 1, dtype=F32)
    pw = jnp.exp(steps[:, None, None, None] * lam_dt[None])
    kd = jnp.real(jnp.einsum("dgop,ndgp,dgpi->ndgio", c_mat, pw[:T], b_bar))
    nb, gpb = S5_LANE_BLOCKS, GROUPS_PER_LANE_BLOCK
    half = gpb * S5_STATE

    lane_group = jnp.arange(LANES) // S5_GROUP_CH
    state_group = jnp.arange(half) // S5_STATE
    rep_ch = jnp.tile(jnp.eye(S5_GROUP_CH, dtype=BF16), (1, gpb))
    rep_st = jnp.tile(jnp.eye(S5_STATE, dtype=BF16), (1, gpb))
    same_ll = (lane_group[:, None] == lane_group[None, :]).astype(BF16)
    same_ls = (lane_group[:, None] == state_group[None, :]).astype(BF16)
    same_sl = same_ls.T

    def spread(x, rep, same, out):
        return jnp.einsum(out, x.astype(BF16), rep, preferred_element_type=BF16) * same

    ar = jnp.arange(T)
    kd = kd.at[0, 0].add(kd[0, 1])
    kb = spread(kd.reshape(T, 2, nb, LANES, S5_GROUP_CH), rep_ch, same_ll, "ndjxo,oq->dnjxq")
    m = jnp.stack([jnp.concatenate([kb[0, b - a] if b >= a else kb[1, a - b] for b in range(T)], axis=-1)
                   for a in range(T)], axis=1).reshape(nb, S5_TK, S5_TK)

    e_in = jnp.stack([T - 1 - ar, ar])
    pw_in = jnp.stack([pw[e_in[d], d] for d in range(2)])
    wc = (pw_in[:, :, :, None, :] * jnp.swapaxes(b_bar, -1, -2)[:, None]).reshape(2, T, nb, LANES, S5_STATE)
    w = jnp.concatenate([spread(part, rep_st, same_ls, "dtjxp,pq->djtxq")
                         for part in (jnp.real(wc), jnp.imag(wc))], axis=-1)
    w = w.reshape(2, nb, S5_TK, S5_STATE_COLS)

    f_out = jnp.stack([ar + 1, T - ar])
    pw_out = jnp.stack([pw[f_out[d], d] for d in range(2)])
    zc = (jnp.swapaxes(c_mat, -1, -2)[:, None] * pw_out[..., None]).reshape(2, T, nb, half, S5_GROUP_CH)
    z = jnp.concatenate([
        jnp.concatenate([spread(part[:, t], rep_ch, same_sl, "djyo,oq->djyq") for t in range(T)], axis=-1)
        for part in (jnp.real(zc), -jnp.imag(zc))], axis=-2)

    a_t = pw[T].reshape(2, nb, half)
    a = jnp.stack([jnp.real(a_t), jnp.imag(a_t)], axis=2)
    return w, m, z, a


def _s5_kernel(u_ref, w_ref, m_ref, z_ref, a_ref, init_ref, y_ref, fin_ref, s_ref, v_ref, spf_ref, spb_ref, *,
               kc, n_tiles):
    ph = pl.program_id(1)
    rt = pl.program_id(2)
    half = S5_STATE_COLS // 2
    rows = kc * BATCH

    @pl.when((ph == 0) & (rt == 0))
    def _():
        s_ref[...] = init_ref[:, 0]

    ucat = jnp.concatenate(
        [u_ref[:, BATCH * t:BATCH * (t + 1), :].reshape(rows, LANES) for t in range(S5_CHUNK)], axis=1)

    def recurrence(d, sp_ref, base):
        v_ref[...] = jnp.dot(ucat, w_ref[0, 0, 0], preferred_element_type=F32)
        a_re = jnp.broadcast_to(a_ref[0, d, 0, 0:1, :], (BATCH, half))
        a_im = jnp.broadcast_to(a_ref[0, d, 0, 1:2, :], (BATCH, half))

        def step(i, carry):
            s_re, s_im = carry
            off = pl.multiple_of((kc - 1 - i if d else i) * BATCH, BATCH)
            dst = pl.multiple_of(base + off, BATCH)
            sp_ref[pl.ds(dst, BATCH), :half] = s_re.astype(BF16)
            sp_ref[pl.ds(dst, BATCH), half:] = s_im.astype(BF16)
            v = v_ref[pl.ds(off, BATCH), :]
            n_re = a_re * s_re - a_im * s_im + v[:, :half]
            n_im = a_re * s_im + a_im * s_re + v[:, half:]
            return n_re, n_im

        s_re, s_im = lax.fori_loop(0, kc, step, (s_ref[d, :, :half], s_ref[d, :, half:]))
        s_ref[d, :, :half] = s_re
        s_ref[d, :, half:] = s_im

        @pl.when(rt == n_tiles - 1)
        def _():
            fin_ref[d, 0] = s_ref[d]

    @pl.when(ph == 0)
    def _():
        recurrence(1, spb_ref, pl.multiple_of((n_tiles - 1 - rt) * rows, rows))

    @pl.when(ph == 1)
    def _():
        recurrence(0, spf_ref, 0)
        s_bwd = spb_ref[pl.ds(pl.multiple_of(rt * rows, rows), rows), :]
        y = (jnp.dot(ucat, m_ref[0, 0], preferred_element_type=F32)
             + jnp.dot(spf_ref[...], z_ref[0, 0, 0], preferred_element_type=F32)
             + jnp.dot(s_bwd, z_ref[0, 1, 0], preferred_element_type=F32))
        for t in range(S5_CHUNK):
            y_ref[:, BATCH * t:BATCH * (t + 1), :] = (
                y[:, LANES * t:LANES * (t + 1)].reshape(kc, BATCH, LANES).astype(BF16))


def _s5_call(u, mats, init, l):
    w, m, z, a = mats
    n = u.shape[0]
    chunk_rows = S5_CHUNK * BATCH
    n_chunks = n // chunk_rows
    kc = min(64, n_chunks)
    n_tiles = n_chunks // kc
    nb = S5_LANE_BLOCKS

    def tile(ph, rt):
        return rt + (1 - ph) * (n_tiles - 1 - 2 * rt)

    both_dirs = lambda shape: pl.BlockSpec((2, 1) + shape, lambda j, ph, rt: (0, j, 0, 0))
    y, fin = pl.pallas_call(
        functools.partial(_s5_kernel, kc=kc, n_tiles=n_tiles),
        grid=(nb, 2, n_tiles),
        in_specs=[pl.BlockSpec((kc, chunk_rows, LANES), lambda j, ph, rt: (tile(ph, rt), 0, j)),
                  pl.BlockSpec((1, 1, 1, S5_TK, S5_STATE_COLS), lambda j, ph, rt: (l, 1 - ph, j, 0, 0)),
                  pl.BlockSpec((1, 1, S5_TK, S5_TK), lambda j, ph, rt: (l, j, 0, 0)),
                  pl.BlockSpec((1, 2, 1, S5_STATE_COLS, S5_TK), lambda j, ph, rt: (l, 0, j, 0, 0)),
                  pl.BlockSpec((1, 2, 1, 2, S5_STATE_COLS // 2), lambda j, ph, rt: (l, 0, j, 0, 0)),
                  both_dirs((BATCH, S5_STATE_COLS))],
        out_specs=[pl.BlockSpec((kc, chunk_rows, LANES), lambda j, ph, rt: (ph * rt, 0, j)),
                   both_dirs((BATCH, S5_STATE_COLS))],
        out_shape=[jax.ShapeDtypeStruct((n_chunks, chunk_rows, D_S5), BF16),
                   jax.ShapeDtypeStruct((2, nb, BATCH, S5_STATE_COLS), F32)],
        scratch_shapes=[pltpu.VMEM((2, BATCH, S5_STATE_COLS), F32),
                        pltpu.VMEM((kc * BATCH, S5_STATE_COLS), F32),
                        pltpu.VMEM((kc * BATCH, S5_STATE_COLS), BF16),
                        pltpu.VMEM((n_chunks * BATCH, S5_STATE_COLS), BF16)],
        compiler_params=_cparams(("arbitrary", "arbitrary", "arbitrary")),
        name="s5",
    )(u.reshape(n_chunks, chunk_rows, D_S5), w, m, z, a, init)
    return y.reshape(n, D_S5), fin


LRU_HALO_PREV = 2 * BATCH
LRU_HALO_NEXT = BATCH


def _lru_kernel(xp_ref, xc_ref, xn_ref, cw_ref, cb_ref, wg_ref, bg_ref, lam_ref, init_ref,
                h_ref, fin_ref, xpad_ref, a_ref, b_ref, hst_ref, *, tm, n_tiles):
    d = pl.program_id(0)
    i = pl.program_id(1)
    te = i + d * (n_tiles - 1 - 2 * i)

    @pl.when(i == 0)
    def _():
        hst_ref[...] = init_ref[0]

    xpad_ref[0:LRU_HALO_PREV, :] = jnp.where(te > 0, xp_ref[...].astype(F32), 0.0)
    xpad_ref[LRU_HALO_PREV:LRU_HALO_PREV + tm, :] = xc_ref[...].astype(F32)
    xpad_ref[LRU_HALO_PREV + tm:, :] = jnp.where(te < n_tiles - 1, xn_ref[...].astype(F32), 0.0)

    xc = cb_ref[0]
    for k in range(4):
        xc = xc + cw_ref[0, k:k + 1, :] * xpad_ref[BATCH * k:BATCH * k + tm, :]

    gates = jnp.tanh(jnp.dot(xc.astype(BF16), wg_ref[0, 0], preferred_element_type=F32) + bg_ref[0, 0])
    z = -lam_ref[0, 0]
    rate = (-0.5 * LRU_C) * (jnp.maximum(z, 0.0) + jnp.log1p(jnp.exp(-jnp.abs(z))))
    log_a = rate * gates[:, :D_LRU] + rate
    a = jnp.exp(log_a)
    half_x = 0.5 * xc
    gated_x = half_x * gates[:, D_LRU:] + half_x
    a_ref[...] = a
    b_ref[...] = jnp.sqrt(-jnp.tanh(log_a) * (a * a + 1.0)) * gated_x

    steps = tm // BATCH

    def step(j, h):
        k = j + d * (steps - 1 - 2 * j)
        off = pl.multiple_of(k * BATCH, BATCH)
        h = a_ref[pl.ds(off, BATCH), :] * h + b_ref[pl.ds(off, BATCH), :]
        h_ref[0, pl.ds(off, BATCH), :] = h.astype(BF16)
        return h

    h = lax.fori_loop(0, steps, step, hst_ref[...], unroll=4)
    hst_ref[...] = h

    @pl.when(i == n_tiles - 1)
    def _():
        fin_ref[0] = h


def _lru_call(xl, conv_w, conv_b, wg, bg, lam, init, l):
    n = xl.shape[0]
    tm = 2048
    n_tiles = n // tm

    def tile(d, i):
        return i + d * (n_tiles - 1 - 2 * i)

    prev_blocks = tm // LRU_HALO_PREV
    next_blocks = tm // LRU_HALO_NEXT
    layer_dir = lambda shape: pl.BlockSpec((1, 1) + shape, lambda d, i: (l, d) + (0,) * len(shape))
    per_dir = lambda shape: pl.BlockSpec((1,) + shape, lambda d, i: (d,) + (0,) * len(shape))
    return pl.pallas_call(
        functools.partial(_lru_kernel, tm=tm, n_tiles=n_tiles),
        grid=(2, n_tiles),
        in_specs=[pl.BlockSpec((LRU_HALO_PREV, D_LRU),
                               lambda d, i: (jnp.maximum(tile(d, i) * prev_blocks - 1, 0), 0)),
                  pl.BlockSpec((tm, D_LRU), lambda d, i: (tile(d, i), 0)),
                  pl.BlockSpec((LRU_HALO_NEXT, D_LRU),
                               lambda d, i: (jnp.minimum((tile(d, i) + 1) * next_blocks, n // LRU_HALO_NEXT - 1), 0)),
                  _layer_spec((4, D_LRU), l), _layer_spec((1, D_LRU), l),
                  layer_dir((D_LRU, 2 * D_LRU)), layer_dir((1, 2 * D_LRU)), layer_dir((1, D_LRU)),
                  per_dir((BATCH, D_LRU))],
        out_specs=[pl.BlockSpec((1, tm, D_LRU), lambda d, i: (d, tile(d, i), 0)),
                   per_dir((BATCH, D_LRU))],
        out_shape=[jax.ShapeDtypeStruct((2, n, D_LRU), BF16),
                   jax.ShapeDtypeStruct((2, BATCH, D_LRU), F32)],
        scratch_shapes=[pltpu.VMEM((LRU_HALO_PREV + tm + LRU_HALO_NEXT, D_LRU), F32),
                        pltpu.VMEM((tm, D_LRU), F32),
                        pltpu.VMEM((tm, D_LRU), F32),
                        pltpu.VMEM((BATCH, D_LRU), F32)],
        compiler_params=_cparams(("arbitrary", "arbitrary")),
        name="lru",
    )(xl, xl, xl, conv_w, conv_b, wg, bg, lam, init)


def _mix_out_kernel(ys_ref, u_ref, hl_ref, gl_ref, xs_ref, d_ref, wglu_ref, bglu_ref, wout_ref,
                    g1_ref, lng_ref, lnb_ref, sh2_ref, sc2_ref, xo_ref, h2_ref):
    g1 = g1_ref[0]
    for r in range(xs_ref.shape[0] // ROW_BLOCK):
        rs = slice(ROW_BLOCK * r, ROW_BLOCK * (r + 1))
        y = (ys_ref[rs, :].astype(F32) + d_ref[0] * u_ref[rs, :].astype(F32)).astype(BF16)
        y = _gelu(y)
        glu = jnp.dot(y, wglu_ref[0], preferred_element_type=F32) + bglu_ref[0]
        y = y * _sigmoid(glu.astype(BF16))
        y_lru = (hl_ref[0, rs, :] + hl_ref[1, rs, :]) * _gelu(gl_ref[rs, :])
        out = jnp.dot(y, wout_ref[0, :D_S5, :], preferred_element_type=F32)
        out = out + jnp.dot(y_lru, wout_ref[0, D_S5:, :], preferred_element_type=F32)
        zsum = DEEPNORM_ALPHA * xs_ref[rs, :] + _per_batch(out, lambda o: o * g1[None])
        xs_new = _ln(zsum) * lng_ref[0] + lnb_ref[0]
        xo_ref[rs, :] = xs_new
        h2_ref[rs, :] = _modulate(xs_new, sh2_ref[0], sc2_ref[0]).astype(BF16)


def _mix_out_call(ys, u, hl, gl, xs, mods, s5_d, w_glu, b_glu, w_out, ln_g, ln_b, l, who):
    n = xs.shape[0]
    tm = 1024
    half = pl.BlockSpec((tm, D_S5), lambda i: (i, 0))
    both = pl.BlockSpec((2, tm, D_S5), lambda i: (0, i, 0))
    row = pl.BlockSpec((tm, D_MODEL), lambda i: (i, 0))
    return pl.pallas_call(
        _mix_out_kernel,
        grid=(n // tm,),
        in_specs=[half, half, both, half, row,
                  _layer_spec((1, D_S5), l), _layer_spec((D_S5, D_S5), l), _layer_spec((1, D_S5), l),
                  _layer_spec((D_MODEL, D_MODEL), l), _mod_spec(l, who, GATE1),
                  _layer_spec((1, D_MODEL), l), _layer_spec((1, D_MODEL), l),
                  _mod_spec(l, who, SHIFT2), _mod_spec(l, who, SCALE2)],
        out_specs=[row, row],
        out_shape=[jax.ShapeDtypeStruct((n, D_MODEL), F32), jax.ShapeDtypeStruct((n, D_MODEL), BF16)],
        compiler_params=_cparams(("parallel",)),
        name="mix_out",
    )(ys, u, hl, gl, xs, s5_d, w_glu, b_glu, w_out, mods, ln_g, ln_b, mods, mods)


def _ffn_up_kernel(h_ref, w_ref, o_ref):
    o_ref[0] = jnp.dot(h_ref[...], w_ref[0], preferred_element_type=F32).astype(BF16)


def _ffn_up_call(h2, w_up, l):
    n = h2.shape[0]
    tm = 1024
    return pl.pallas_call(
        _ffn_up_kernel,
        grid=(2, n // tm),
        in_specs=[pl.BlockSpec((tm, D_MODEL), lambda s, i: (i, 0)),
                  pl.BlockSpec((1, D_MODEL, D_FF), lambda s, i: (l, 0, s))],
        out_specs=pl.BlockSpec((1, tm, D_FF), lambda s, i: (s, i, 0)),
        out_shape=jax.ShapeDtypeStruct((2, n, D_FF), BF16),
        compiler_params=_cparams(("parallel", "parallel")),
        name="ffn_up",
    )(h2, w_up)


GRID_ROW = GRID_W * BATCH
EPILOGUE_ROWS = 256
CONV_ROWS = 512
_GELU_K = 0.7978845608028654


def _half_gelu_gate(h, v):
    hv = h * v
    return hv + hv * jnp.tanh(h * (2.0 * _GELU_K + (8.0 * 0.044715 * _GELU_K) * (h * h)))


def _ffn_down_kernel(u_ref, v_ref, cw_ref, cb_ref, wd_ref, xs_ref, g2_ref, lng_ref, lnb_ref,
                     xo_ref, win_ref, act_ref, *acc, n_tiles, on_grid):
    s = pl.program_id(0)
    c = pl.program_id(1)
    tm = GRID_ROW
    win = win_ref.at[c]
    acc_ref = acc[0] if acc else xo_ref

    @pl.when(s == 0)
    def _():
        win[0] = jnp.zeros((tm, FF_CHUNK), BF16)

    def finish_tile(accumulate):
        edge = jnp.zeros((BATCH, LANES), BF16)
        for r in range(tm // CONV_ROWS):
            r0, r1 = CONV_ROWS * r, CONV_ROWS * (r + 1)
            lo, hi = max(r0 - BATCH, 0), min(r1 + BATCH, tm)
            for k in range(FF_CHUNK // LANES):
                ls = slice(LANES * k, LANES * (k + 1))
                w = [(0.5 * cw_ref[0, t:t + 1, ls]).astype(BF16) for t in range(9)]
                nxt = lambda a, b: jnp.where(s < n_tiles, u_ref[0, a:b, ls], jnp.zeros((), BF16))
                if on_grid:
                    slabs = ((-1, win[0, lo:hi, ls], edge, edge), (0, win[1, lo:hi, ls], edge, edge),
                             (1, nxt(lo, hi), edge, edge))
                else:
                    slabs = ((0, win[1, lo:hi, ls], win[0, tm - BATCH:, ls], nxt(0, BATCH)),)
                conv = (0.5 * cb_ref[0, :, ls]).astype(BF16)
                for dy, ext, before, after in slabs:
                    if r0 == 0:
                        ext = jnp.concatenate([before, ext], axis=0)
                    if r1 == tm:
                        ext = jnp.concatenate([ext, after], axis=0)
                    t0 = 3 * (dy + 1)
                    conv = conv + (w[t0] * ext[:CONV_ROWS] + w[t0 + 1] * ext[BATCH:BATCH + CONV_ROWS]
                                   + w[t0 + 2] * ext[2 * BATCH:])
                act_ref[r0:r1, ls] = _half_gelu_gate(conv, v_ref[0, r0:r1, ls])
            part = jnp.dot(act_ref[r0:r1, :], wd_ref[0], preferred_element_type=F32)
            if accumulate:
                acc_ref[r0:r1, :] += part
            else:
                acc_ref[r0:r1, :] = part
        win[0] = win[1]

    @pl.when((s > 0) & (c == 0))
    def _():
        finish_tile(False)

    @pl.when((s > 0) & (c == 1))
    def _():
        finish_tile(True)
        g2 = g2_ref[0]
        steps = EPILOGUE_ROWS // BATCH
        for r in range(tm // EPILOGUE_ROWS):
            rs = slice(EPILOGUE_ROWS * r, EPILOGUE_ROWS * (r + 1))
            zsum = DEEPNORM_ALPHA * xs_ref[rs, :] + _per_batch(acc_ref[rs, :], lambda o: o * g2[None])
            res = _ln(zsum) * lng_ref[0] + lnb_ref[0]
            if acc:
                for t in range(steps):
                    xo_ref[:, steps * r + t, :] = res[BATCH * t:BATCH * (t + 1), :]
            else:
                xo_ref[rs, :] = res

    win[1] = u_ref[0]


def _ffn_down_call(uv, xs, mods, conv_w, conv_b, w_down, ln_g, ln_b, l, who, on_grid, batch_major_out=False):
    n = xs.shape[0]
    tm = GRID_ROW
    n_tiles = n // tm
    n_chunks = D_FF // FF_CHUNK
    assert n_chunks == 2
    done = lambda s: jnp.maximum(s - 1, 0)
    row = pl.BlockSpec((tm, D_MODEL), lambda s, c: (done(s), 0))
    scratch = [pltpu.VMEM((n_chunks, 2, tm, FF_CHUNK), BF16), pltpu.VMEM((tm, FF_CHUNK), BF16)]
    out_spec, out_shape = row, jax.ShapeDtypeStruct((n, D_MODEL), F32)
    if batch_major_out:
        scratch.append(pltpu.VMEM((tm, D_MODEL), F32))
        out_spec = pl.BlockSpec((BATCH, tm // BATCH, D_MODEL), lambda s, c: (0, done(s), 0))
        out_shape = jax.ShapeDtypeStruct((BATCH, n // BATCH, D_MODEL), F32)
    return pl.pallas_call(
        functools.partial(_ffn_down_kernel, n_tiles=n_tiles, on_grid=on_grid),
        grid=(n_tiles + 1, n_chunks),
        in_specs=[pl.BlockSpec((1, tm, FF_CHUNK), lambda s, c: (0, jnp.minimum(s, n_tiles - 1), c)),
                  pl.BlockSpec((1, tm, FF_CHUNK), lambda s, c: (1, done(s), c)),
                  pl.BlockSpec((1, 9, FF_CHUNK), lambda s, c: (l, 0, c)),
                  pl.BlockSpec((1, 1, FF_CHUNK), lambda s, c: (l, 0, c)),
                  pl.BlockSpec((1, FF_CHUNK, D_MODEL), lambda s, c: (l, c, 0)),
                  row, _mod_spec(l, who, GATE2), _layer_spec((1, D_MODEL), l), _layer_spec((1, D_MODEL), l)],
        out_specs=out_spec,
        out_shape=out_shape,
        scratch_shapes=scratch,
        compiler_params=_cparams(("arbitrary", "arbitrary")),
        name="ffn_down_grid" if on_grid else "ffn_down_seq",
    )(uv, uv, conv_w, conv_b, w_down, xs, mods, ln_g, ln_b)


def kernel(x, c, ctx, c_ctx, w_mod, b_mod, w_in, s5_lam_re, s5_lam_im, s5_log_dt, s5_b_re, s5_b_im, s5_c_re, s5_c_im, s5_d, s5_w_glu, s5_b_glu, lru_conv_w, lru_conv_b, lru_w_a, lru_b_a, lru_w_x, lru_b_x, lru_lam, w_out, ln1_g, ln1_b, ffn_w_up, ffn_conv_w, ffn_conv_b, ffn_w_down, ln2_g, ln2_b):
    xs, cs = x, ctx

    c_all = jnp.concatenate([c, jnp.broadcast_to(c_ctx, (BATCH, D_MODEL))], axis=0)
    mods = _mod_call(c_all, w_mod, b_mod)

    row = lambda v: v.reshape(DEPTH, 1, v.shape[-1])
    w_in_b = w_in.astype(BF16)
    mats = jax.vmap(_s5_matrices)(s5_lam_re, s5_lam_im, s5_log_dt, s5_b_re, s5_b_im, s5_c_re, s5_c_im)
    eye_h = jnp.eye(LRU_HEADS, dtype=F32)
    dense = lambda wh: jnp.einsum("ldhij,hk->ldhikj", wh, eye_h).reshape(DEPTH, 2, D_LRU, D_LRU)
    wg = (0.5 * jnp.concatenate([dense(lru_w_a), dense(lru_w_x)], axis=-1)).astype(BF16)
    bg = 0.5 * jnp.concatenate([lru_b_a, lru_b_x], axis=-1).reshape(DEPTH, 2, 1, 2 * D_LRU)
    lam = lru_lam.reshape(DEPTH, 2, 1, D_LRU)
    w_glu = s5_w_glu.astype(BF16)
    w_out_b = w_out.astype(BF16)
    w_up = ffn_w_up.astype(BF16)
    f_conv_w = ffn_conv_w.reshape(DEPTH, 9, D_FF)
    w_down = ffn_w_down.astype(BF16)
    ln1 = (row(ln1_g), row(ln1_b))
    ln2 = (row(ln2_g), row(ln2_b))

    s5_zero = jnp.zeros((2, S5_LANE_BLOCKS, BATCH, S5_STATE_COLS), F32)
    lru_zero = jnp.zeros((2, BATCH, D_LRU), F32)

    for l in range(DEPTH):
        last = l == DEPTH - 1

        def mixers(stream, who, s5_init, lru_init):
            u, xl, gl, *tm_copy = _mix_in_call(stream, mods, w_in_b, l, who, batch_major=(l == 0))
            ys, s5_fin = _s5_call(u, mats, s5_init, l)
            hl, lru_fin = _lru_call(xl, lru_conv_w, row(lru_conv_b), wg, bg, lam, lru_init, l)
            return (ys, u, hl, gl), (s5_fin, lru_fin), (tm_copy[0] if tm_copy else stream)

        def finish(stream, mixed, who, on_grid):
            xs_mid, h2 = _mix_out_call(*mixed, stream, mods, row(s5_d), w_glu, row(s5_b_glu), w_out_b, *ln1, l, who)
            uv = _ffn_up_call(h2, w_up, l)
            return _ffn_down_call(uv, xs_mid, mods, f_conv_w, row(ffn_conv_b), w_down, *ln2, l, who, on_grid,
                                  batch_major_out=last)

        mixed_c, (s5_fin, lru_fin), cs = mixers(cs, 1, s5_zero, lru_zero)
        mixed_x, _, xs = mixers(xs, 0, s5_fin, lru_fin)
        xs = finish(xs, mixed_x, 0, True)
        if not last:
            cs = finish(cs, mixed_c, 1, False)

    return xs
```

```python
import functools

import jax
import jax.numpy as jnp
from jax import lax
from jax.experimental import pallas as pl
from jax.experimental.pallas import tpu as pltpu

F32 = jnp.float32
BF16 = jnp.bfloat16

D_MODEL = 1024
BATCH = 16
DEPTH = 4
GRID_W = 64
D_S5 = 512
D_LRU = 512
S5_GROUP_CH = 16
S5_GROUPS = 32
S5_STATE = 64
LRU_HEADS = 8
LRU_HEAD_DIM = 64
LRU_C = 8.0
D_FF = 2816
N_MOD = 6
LN_EPS = 1e-6
DEEPNORM_ALPHA = (2 * DEPTH) ** 0.25

LANES = 128
GROUPS_PER_LANE_BLOCK = LANES // S5_GROUP_CH
S5_LANE_BLOCKS = D_S5 // LANES
S5_STATE_COLS = 2 * GROUPS_PER_LANE_BLOCK * S5_STATE
S5_CHUNK = 8
S5_TK = S5_CHUNK * LANES
MOD_ROWS = 2 * BATCH
ROW_BLOCK = 256
FF_CHUNK = 1408
VMEM_LIMIT = 60 * 1024 * 1024
SHIFT1, SCALE1, GATE1, SHIFT2, SCALE2, GATE2 = range(N_MOD)


def _cparams(sem):
    return pltpu.CompilerParams(dimension_semantics=sem, vmem_limit_bytes=VMEM_LIMIT)


def _layer_spec(shape, l):
    return pl.BlockSpec((1,) + shape, lambda *_: (l,) + (0,) * len(shape))


def _mod_spec(l, who, slot):
    return pl.BlockSpec((1, BATCH, D_MODEL), lambda *_: (l, who, slot))


def _sigmoid(x):
    return 0.5 * jnp.tanh(0.5 * x) + 0.5


def _gelu(x):
    return 0.5 * x * (1.0 + jnp.tanh(0.7978845608028654 * (x + 0.044715 * (x * x * x))))


def _ln(x):
    mu = jnp.mean(x, axis=-1, keepdims=True)
    xc = x - mu
    var = jnp.mean(xc * xc, axis=-1, keepdims=True)
    return xc * lax.rsqrt(var + LN_EPS)


def _per_batch(x, fn):
    rows, d = x.shape
    return fn(x.reshape(rows // BATCH, BATCH, d)).reshape(rows, d)


def _modulate(x, shift, scale):
    return _per_batch(_ln(x), lambda h: h * (1.0 + scale)[None] + shift[None])


def _mod_kernel(c_ref, w_ref, b_ref, o_ref):
    c = c_ref[...]
    s = c * _sigmoid(c)
    o_ref[0] = jnp.dot(s, w_ref[0], preferred_element_type=F32) + b_ref[0]


def _mod_call(c_all, w_mod, b_mod):
    tn = 1536
    return pl.pallas_call(
        _mod_kernel,
        grid=(DEPTH, N_MOD * D_MODEL // tn),
        in_specs=[pl.BlockSpec((MOD_ROWS, D_MODEL), lambda l, j: (0, 0)),
                  pl.BlockSpec((1, D_MODEL, tn), lambda l, j: (l, 0, j)),
                  pl.BlockSpec((1, 1, tn), lambda l, j: (l, 0, j))],
        out_specs=pl.BlockSpec((1, MOD_ROWS, tn), lambda l, j: (l, 0, j)),
        out_shape=jax.ShapeDtypeStruct((DEPTH, MOD_ROWS, N_MOD * D_MODEL), F32),
        compiler_params=_cparams(("parallel", "parallel")),
        name="mod",
    )(c_all, w_mod, b_mod.reshape(DEPTH, 1, N_MOD * D_MODEL))


def _mix_in_kernel(x_ref, sh_ref, sc_ref, w_ref, u_ref, xl_ref, gl_ref, *xs_out):
    rows = u_ref.shape[0]
    steps = ROW_BLOCK // BATCH
    for r in range(rows // ROW_BLOCK):
        rs = slice(ROW_BLOCK * r, ROW_BLOCK * (r + 1))
        if xs_out:
            x = jnp.concatenate([x_ref[:, steps * r + t, :] for t in range(steps)], axis=0)
            xs_out[0][rs, :] = x
        else:
            x = x_ref[rs, :]
        h = _modulate(x, sh_ref[0], sc_ref[0]).astype(BF16)
        p = jnp.dot(h, w_ref[0], preferred_element_type=F32)
        u_ref[rs, :] = p[:, :D_S5].astype(BF16)
        xl_ref[rs, :] = p[:, D_S5:D_S5 + D_LRU].astype(BF16)
        gl_ref[rs, :] = p[:, D_S5 + D_LRU:].astype(BF16)


def _mix_in_call(xs, mods, w_in, l, who, batch_major=False):
    n = xs.shape[0] * xs.shape[1] if batch_major else xs.shape[0]
    tm = 1024
    out = pl.BlockSpec((tm, D_S5), lambda i: (i, 0))
    out_specs = [out, out, out]
    out_shape = [jax.ShapeDtypeStruct((n, D_S5), BF16)] * 3
    x_spec = pl.BlockSpec((tm, D_MODEL), lambda i: (i, 0))
    if batch_major:
        x_spec = pl.BlockSpec((BATCH, tm // BATCH, D_MODEL), lambda i: (0, i, 0))
        out_specs.append(pl.BlockSpec((tm, D_MODEL), lambda i: (i, 0)))
        out_shape.append(jax.ShapeDtypeStruct((n, D_MODEL), F32))
    return pl.pallas_call(
        _mix_in_kernel,
        grid=(n // tm,),
        in_specs=[x_spec, _mod_spec(l, who, SHIFT1), _mod_spec(l, who, SCALE1),
                  _layer_spec((D_MODEL, D_S5 + 2 * D_LRU), l)],
        out_specs=out_specs,
        out_shape=out_shape,
        compiler_params=_cparams(("parallel",)),
        name="mix_in",
    )(xs, mods, mods, w_in)


def _s5_matrices(lam_re, lam_im, log_dt, b_re, b_im, c_re, c_im):
    T = S5_CHUNK
    lam = lax.complex(lam_re, lam_im)
    lam_dt = lam * jnp.exp(log_dt)[..., None]
    lam_bar = jnp.exp(lam_dt)
    b_bar = ((lam_bar - 1.0) / lam)[..., None] * lax.complex(b_re, b_im)
    c_mat = lax.complex(c_re, c_im)
    steps = jnp.arange(T + 1, dtype=F32)
    pw = jnp.exp(steps[:, None, None, None] * lam_dt[None])
    kd = jnp.real(jnp.einsum("dgop,ndgp,dgpi->ndgio", c_mat, pw[:T], b_bar))
    nb, gpb = S5_LANE_BLOCKS, GROUPS_PER_LANE_BLOCK
    half = gpb * S5_STATE

    lane_group = jnp.arange(LANES) // S5_GROUP_CH
    state_group = jnp.arange(half) // S5_STATE
    rep_ch = jnp.tile(jnp.eye(S5_GROUP_CH, dtype=BF16), (1, gpb))
    rep_st = jnp.tile(jnp.eye(S5_STATE, dtype=BF16), (1, gpb))
    same_ll = (lane_group[:, None] == lane_group[None, :]).astype(BF16)
    same_ls = (lane_group[:, None] == state_group[None, :]).astype(BF16)
    same_sl = same_ls.T

    def spread(x, rep, same, out):
        return jnp.einsum(out, x.astype(BF16), rep, preferred_element_type=BF16) * same

    ar = jnp.arange(T)
    kd = kd.at[0, 0].add(kd[0, 1])
    kb = spread(kd.reshape(T, 2, nb, LANES, S5_GROUP_CH), rep_ch, same_ll, "ndjxo,oq->dnjxq")
    m = jnp.stack([jnp.concatenate([kb[0, b - a] if b >= a else kb[1, a - b] for b in range(T)], axis=-1)
                   for a in range(T)], axis=1).reshape(nb, S5_TK, S5_TK)

    e_in = jnp.stack([T - 1 - ar, ar])
    pw_in = jnp.stack([pw[e_in[d], d] for d in range(2)])
    wc = (pw_in[:, :, :, None, :] * jnp.swapaxes(b_bar, -1, -2)[:, None]).reshape(2, T, nb, LANES, S5_STATE)
    w = jnp.concatenate([spread(part, rep_st, same_ls, "dtjxp,pq->djtxq")
                         for part in (jnp.real(wc), jnp.imag(wc))], axis=-1)
    w = w.reshape(2, nb, S5_TK, S5_STATE_COLS)

    f_out = jnp.stack([ar + 1, T - ar])
    pw_out = jnp.stack([pw[f_out[d], d] for d in range(2)])
    zc = (jnp.swapaxes(c_mat, -1, -2)[:, None] * pw_out[..., None]).reshape(2, T, nb, half, S5_GROUP_CH)
    z = jnp.concatenate([
        jnp.concatenate([spread(part[:, t], rep_ch, same_sl, "djyo,oq->djyq") for t in range(T)], axis=-1)
        for part in (jnp.real(zc), -jnp.imag(zc))], axis=-2)

    a_t = pw[T].reshape(2, nb, half)
    a = jnp.stack([jnp.real(a_t), jnp.imag(a_t)], axis=2)
    return w, m, z, a


def _s5_kernel(u_ref, w_ref, m_ref, z_ref, a_ref, init_ref, y_ref, fin_ref, s_ref, v_ref, spf_ref, spb_ref, *,
               kc, n_tiles):
    ph = pl.program_id(1)
    rt = pl.program_id(2)
    half = S5_STATE_COLS // 2
    rows = kc * BATCH

    @pl.when((ph == 0) & (rt == 0))
    def _():
        s_ref[...] = init_ref[:, 0]

    ucat = jnp.concatenate(
        [u_ref[:, BATCH * t:BATCH * (t + 1), :].reshape(rows, LANES) for t in range(S5_CHUNK)], axis=1)

    def recurrence(d, sp_ref, base):
        v_ref[...] = jnp.dot(ucat, w_ref[0, 0, 0], preferred_element_type=F32)
        a_re = jnp.broadcast_to(a_ref[0, d, 0, 0:1, :], (BATCH, half))
        a_im = jnp.broadcast_to(a_ref[0, d, 0, 1:2, :], (BATCH, half))

        def step(i, carry):
            s_re, s_im = carry
            off = pl.multiple_of((kc - 1 - i if d else i) * BATCH, BATCH)
            dst = pl.multiple_of(base + off, BATCH)
            sp_ref[pl.ds(dst, BATCH), :half] = s_re.astype(BF16)
            sp_ref[pl.ds(dst, BATCH), half:] = s_im.astype(BF16)
            v = v_ref[pl.ds(off, BATCH), :]
            n_re = a_re * s_re - a_im * s_im + v[:, :half]
            n_im = a_re * s_im + a_im * s_re + v[:, half:]
            return n_re, n_im

        s_re, s_im = lax.fori_loop(0, kc, step, (s_ref[d, :, :half], s_ref[d, :, half:]))
        s_ref[d, :, :half] = s_re
        s_ref[d, :, half:] = s_im

        @pl.when(rt == n_tiles - 1)
        def _():
            fin_ref[d, 0] = s_ref[d]

    @pl.when(ph == 0)
    def _():
        recurrence(1, spb_ref, pl.multiple_of((n_tiles - 1 - rt) * rows, rows))

    @pl.when(ph == 1)
    def _():
        recurrence(0, spf_ref, 0)
        s_bwd = spb_ref[pl.ds(pl.multiple_of(rt * rows, rows), rows), :]
        y = (jnp.dot(ucat, m_ref[0, 0], preferred_element_type=F32)
             + jnp.dot(spf_ref[...], z_ref[0, 0, 0], preferred_element_type=F32)
             + jnp.dot(s_bwd, z_ref[0, 1, 0], preferred_element_type=F32))
        for t in range(S5_CHUNK):
            y_ref[:, BATCH * t:BATCH * (t + 1), :] = (
                y[:, LANES * t:LANES * (t + 1)].reshape(kc, BATCH, LANES).astype(BF16))


def _s5_call(u, mats, init, l):
    w, m, z, a = mats
    n = u.shape[0]
    chunk_rows = S5_CHUNK * BATCH
    n_chunks = n // chunk_rows
    kc = min(64, n_chunks)
    n_tiles = n_chunks // kc
    nb = S5_LANE_BLOCKS

    def tile(ph, rt):
        return rt + (1 - ph) * (n_tiles - 1 - 2 * rt)

    both_dirs = lambda shape: pl.BlockSpec((2, 1) + shape, lambda j, ph, rt: (0, j, 0, 0))
    y, fin = pl.pallas_call(
        functools.partial(_s5_kernel, kc=kc, n_tiles=n_tiles),
        grid=(nb, 2, n_tiles),
        in_specs=[pl.BlockSpec((kc, chunk_rows, LANES), lambda j, ph, rt: (tile(ph, rt), 0, j)),
                  pl.BlockSpec((1, 1, 1, S5_TK, S5_STATE_COLS), lambda j, ph, rt: (l, 1 - ph, j, 0, 0)),
                  pl.BlockSpec((1, 1, S5_TK, S5_TK), lambda j, ph, rt: (l, j, 0, 0)),
                  pl.BlockSpec((1, 2, 1, S5_STATE_COLS, S5_TK), lambda j, ph, rt: (l, 0, j, 0, 0)),
                  pl.BlockSpec((1, 2, 1, 2, S5_STATE_COLS // 2), lambda j, ph, rt: (l, 0, j, 0, 0)),
                  both_dirs((BATCH, S5_STATE_COLS))],
        out_specs=[pl.BlockSpec((kc, chunk_rows, LANES), lambda j, ph, rt: (ph * rt, 0, j)),
                   both_dirs((BATCH, S5_STATE_COLS))],
        out_shape=[jax.ShapeDtypeStruct((n_chunks, chunk_rows, D_S5), BF16),
                   jax.ShapeDtypeStruct((2, nb, BATCH, S5_STATE_COLS), F32)],
        scratch_shapes=[pltpu.VMEM((2, BATCH, S5_STATE_COLS), F32),
                        pltpu.VMEM((kc * BATCH, S5_STATE_COLS), F32),
                        pltpu.VMEM((kc * BATCH, S5_STATE_COLS), BF16),
                        pltpu.VMEM((n_chunks * BATCH, S5_STATE_COLS), BF16)],
        compiler_params=_cparams(("arbitrary", "arbitrary", "arbitrary")),
        name="s5",
    )(u.reshape(n_chunks, chunk_rows, D_S5), w, m, z, a, init)
    return y.reshape(n, D_S5), fin


LRU_HALO_PREV = 2 * BATCH
LRU_HALO_NEXT = BATCH


def _lru_kernel(xp_ref, xc_ref, xn_ref, cw_ref, cb_ref, wg_ref, bg_ref, lam_ref, init_ref,
                h_ref, fin_ref, xpad_ref, a_ref, b_ref, hst_ref, *, tm, n_tiles):
    d = pl.program_id(0)
    i = pl.program_id(1)
    te = i + d * (n_tiles - 1 - 2 * i)

    @pl.when(i == 0)
    def _():
        hst_ref[...] = init_ref[0]

    zero = jnp.zeros((), BF16)
    xpad_ref[0:LRU_HALO_PREV, :] = jnp.where(te > 0, xp_ref[...], zero)
    xpad_ref[LRU_HALO_PREV:LRU_HALO_PREV + tm, :] = xc_ref[...]
    xpad_ref[LRU_HALO_PREV + tm:, :] = jnp.where(te < n_tiles - 1, xn_ref[...], zero)

    xc_b = cb_ref[0].astype(BF16)
    for k in range(4):
        xc_b = xc_b + cw_ref[0, k:k + 1, :].astype(BF16) * xpad_ref[BATCH * k:BATCH * k + tm, :]
    xc = xc_b.astype(F32)

    gates = jnp.tanh(jnp.dot(xc_b, wg_ref[0, 0], preferred_element_type=F32) + bg_ref[0, 0])
    z = -lam_ref[0, 0]
    rate = (-0.5 * LRU_C) * (jnp.maximum(z, 0.0) + jnp.log1p(jnp.exp(-jnp.abs(z))))
    log_a = rate * gates[:, :D_LRU] + rate
    a = jnp.exp(log_a)
    half_x = 0.5 * xc
    gated_x = half_x * gates[:, D_LRU:] + half_x
    a_ref[...] = a
    b_ref[...] = jnp.sqrt(-jnp.tanh(log_a) * (a * a + 1.0)) * gated_x

    steps = tm // BATCH

    def step(j, h):
        k = j + d * (steps - 1 - 2 * j)
        off = pl.multiple_of(k * BATCH, BATCH)
        h = a_ref[pl.ds(off, BATCH), :] * h + b_ref[pl.ds(off, BATCH), :]
        h_ref[0, pl.ds(off, BATCH), :] = h.astype(BF16)
        return h

    h = lax.fori_loop(0, steps, step, hst_ref[...], unroll=4)
    hst_ref[...] = h

    @pl.when(i == n_tiles - 1)
    def _():
        fin_ref[0] = h


def _lru_call(xl, conv_w, conv_b, wg, bg, lam, init, l):
    n = xl.shape[0]
    tm = 2048
    n_tiles = n // tm

    def tile(d, i):
        return i + d * (n_tiles - 1 - 2 * i)

    prev_blocks = tm // LRU_HALO_PREV
    next_blocks = tm // LRU_HALO_NEXT
    layer_dir = lambda shape: pl.BlockSpec((1, 1) + shape, lambda d, i: (l, d) + (0,) * len(shape))
    per_dir = lambda shape: pl.BlockSpec((1,) + shape, lambda d, i: (d,) + (0,) * len(shape))
    return pl.pallas_call(
        functools.partial(_lru_kernel, tm=tm, n_tiles=n_tiles),
        grid=(2, n_tiles),
        in_specs=[pl.BlockSpec((LRU_HALO_PREV, D_LRU),
                               lambda d, i: (jnp.maximum(tile(d, i) * prev_blocks - 1, 0), 0)),
                  pl.BlockSpec((tm, D_LRU), lambda d, i: (tile(d, i), 0)),
                  pl.BlockSpec((LRU_HALO_NEXT, D_LRU),
                               lambda d, i: (jnp.minimum((tile(d, i) + 1) * next_blocks, n // LRU_HALO_NEXT - 1), 0)),
                  _layer_spec((4, D_LRU), l), _layer_spec((1, D_LRU), l),
                  layer_dir((D_LRU, 2 * D_LRU)), layer_dir((1, 2 * D_LRU)), layer_dir((1, D_LRU)),
                  per_dir((BATCH, D_LRU))],
        out_specs=[pl.BlockSpec((1, tm, D_LRU), lambda d, i: (d, tile(d, i), 0)),
                   per_dir((BATCH, D_LRU))],
        out_shape=[jax.ShapeDtypeStruct((2, n, D_LRU), BF16),
                   jax.ShapeDtypeStruct((2, BATCH, D_LRU), F32)],
        scratch_shapes=[pltpu.VMEM((LRU_HALO_PREV + tm + LRU_HALO_NEXT, D_LRU), BF16),
                        pltpu.VMEM((tm, D_LRU), F32),
                        pltpu.VMEM((tm, D_LRU), F32),
                        pltpu.VMEM((BATCH, D_LRU), F32)],
        compiler_params=_cparams(("arbitrary", "arbitrary")),
        name="lru",
    )(xl, xl, xl, conv_w, conv_b, wg, bg, lam, init)


def _mix_out_kernel(ys_ref, u_ref, hl_ref, gl_ref, xs_ref, d_ref, wglu_ref, bglu_ref, wout_ref,
                    g1_ref, lng_ref, lnb_ref, xo_ref):
    g1 = g1_ref[0]
    for r in range(xs_ref.shape[0] // ROW_BLOCK):
        rs = slice(ROW_BLOCK * r, ROW_BLOCK * (r + 1))
        y = (ys_ref[rs, :].astype(F32) + d_ref[0] * u_ref[rs, :].astype(F32)).astype(BF16)
        y = _gelu(y)
        glu = jnp.dot(y, wglu_ref[0], preferred_element_type=F32) + bglu_ref[0]
        y = y * _sigmoid(glu.astype(BF16))
        y_lru = (hl_ref[0, rs, :] + hl_ref[1, rs, :]) * _gelu(gl_ref[rs, :])
        out = jnp.dot(y, wout_ref[0, :D_S5, :], preferred_element_type=F32)
        out = out + jnp.dot(y_lru, wout_ref[0, D_S5:, :], preferred_element_type=F32)
        zsum = DEEPNORM_ALPHA * xs_ref[rs, :] + _per_batch(out, lambda o: o * g1[None])
        xo_ref[rs, :] = _ln(zsum) * lng_ref[0] + lnb_ref[0]


def _mix_out_call(ys, u, hl, gl, xs, mods, s5_d, w_glu, b_glu, w_out, ln_g, ln_b, l, who):
    n = xs.shape[0]
    tm = 1024
    half = pl.BlockSpec((tm, D_S5), lambda i: (i, 0))
    both = pl.BlockSpec((2, tm, D_S5), lambda i: (0, i, 0))
    row = pl.BlockSpec((tm, D_MODEL), lambda i: (i, 0))
    return pl.pallas_call(
        _mix_out_kernel,
        grid=(n // tm,),
        in_specs=[half, half, both, half, row,
                  _layer_spec((1, D_S5), l), _layer_spec((D_S5, D_S5), l), _layer_spec((1, D_S5), l),
                  _layer_spec((D_MODEL, D_MODEL), l), _mod_spec(l, who, GATE1),
                  _layer_spec((1, D_MODEL), l), _layer_spec((1, D_MODEL), l)],
        out_specs=row,
        out_shape=jax.ShapeDtypeStruct((n, D_MODEL), F32),
        compiler_params=_cparams(("parallel",)),
        name="mix_out",
    )(ys, u, hl, gl, xs, s5_d, w_glu, b_glu, w_out, mods, ln_g, ln_b)


UP_ROWS = 512


def _ffn_up_kernel(x_ref, sh_ref, sc_ref, w_ref, o_ref):
    for r in range(x_ref.shape[0] // UP_ROWS):
        rs = slice(UP_ROWS * r, UP_ROWS * (r + 1))
        h = _modulate(x_ref[rs, :], sh_ref[0], sc_ref[0]).astype(BF16)
        for half in range(2):
            o_ref[half, rs, :] = jnp.dot(h, w_ref[0, :, D_FF * half:D_FF * (half + 1)],
                                         preferred_element_type=F32).astype(BF16)


def _ffn_up_call(xs, mods, w_up, l, who):
    n = xs.shape[0]
    tm = 1024
    return pl.pallas_call(
        _ffn_up_kernel,
        grid=(n // tm,),
        in_specs=[pl.BlockSpec((tm, D_MODEL), lambda i: (i, 0)),
                  _mod_spec(l, who, SHIFT2), _mod_spec(l, who, SCALE2),
                  pl.BlockSpec((1, D_MODEL, 2 * D_FF), lambda i: (l, 0, 0), pipeline_mode=pl.Buffered(1))],
        out_specs=pl.BlockSpec((2, tm, D_FF), lambda i: (0, i, 0)),
        out_shape=jax.ShapeDtypeStruct((2, n, D_FF), BF16),
        compiler_params=_cparams(("parallel",)),
        name="ffn_up",
    )(xs, mods, mods, w_up)


GRID_ROW = GRID_W * BATCH
EPILOGUE_ROWS = 256
CONV_ROWS = 512
_GELU_K = 0.7978845608028654


def _half_gelu_gate(h, v):
    hv = h * v
    return hv + hv * jnp.tanh(h * (2.0 * _GELU_K + (8.0 * 0.044715 * _GELU_K) * (h * h)))


def _ffn_down_kernel(u_ref, v_ref, cw_ref, cb_ref, wd_ref, xs_ref, g2_ref, lng_ref, lnb_ref,
                     xo_ref, win_ref, act_ref, *acc, n_tiles, on_grid):
    s = pl.program_id(0)
    c = pl.program_id(1)
    tm = GRID_ROW
    win = win_ref.at[c]
    acc_ref = acc[0] if acc else xo_ref

    @pl.when(s == 0)
    def _():
        win[0] = jnp.zeros((tm, FF_CHUNK), BF16)

    def finish_tile(accumulate):
        edge = jnp.zeros((BATCH, LANES), BF16)
        for r in range(tm // CONV_ROWS):
            r0, r1 = CONV_ROWS * r, CONV_ROWS * (r + 1)
            lo, hi = max(r0 - BATCH, 0), min(r1 + BATCH, tm)
            for k in range(FF_CHUNK // LANES):
                ls = slice(LANES * k, LANES * (k + 1))
                w = [(0.5 * cw_ref[0, t:t + 1, ls]).astype(BF16) for t in range(9)]
                nxt = lambda a, b: jnp.where(s < n_tiles, u_ref[0, a:b, ls], jnp.zeros((), BF16))
                if on_grid:
                    slabs = ((-1, win[0, lo:hi, ls], edge, edge), (0, win[1, lo:hi, ls], edge, edge),
                             (1, nxt(lo, hi), edge, edge))
                else:
                    slabs = ((0, win[1, lo:hi, ls], win[0, tm - BATCH:, ls], nxt(0, BATCH)),)
                conv = (0.5 * cb_ref[0, :, ls]).astype(BF16)
                for dy, ext, before, after in slabs:
                    if r0 == 0:
                        ext = jnp.concatenate([before, ext], axis=0)
                    if r1 == tm:
                        ext = jnp.concatenate([ext, after], axis=0)
                    t0 = 3 * (dy + 1)
                    conv = conv + (w[t0] * ext[:CONV_ROWS] + w[t0 + 1] * ext[BATCH:BATCH + CONV_ROWS]
                                   + w[t0 + 2] * ext[2 * BATCH:])
                act_ref[r0:r1, ls] = _half_gelu_gate(conv, v_ref[0, r0:r1, ls])
            part = jnp.dot(act_ref[r0:r1, :], wd_ref[0], preferred_element_type=F32)
            if accumulate:
                acc_ref[r0:r1, :] += part
            else:
                acc_ref[r0:r1, :] = part
        win[0] = win[1]

    @pl.when((s > 0) & (c == 0))
    def _():
        finish_tile(False)

    @pl.when((s > 0) & (c == 1))
    def _():
        finish_tile(True)
        g2 = g2_ref[0]
        steps = EPILOGUE_ROWS // BATCH
        for r in range(tm // EPILOGUE_ROWS):
            rs = slice(EPILOGUE_ROWS * r, EPILOGUE_ROWS * (r + 1))
            zsum = DEEPNORM_ALPHA * xs_ref[rs, :] + _per_batch(acc_ref[rs, :], lambda o: o * g2[None])
            res = _ln(zsum) * lng_ref[0] + lnb_ref[0]
            if acc:
                for t in range(steps):
                    xo_ref[:, steps * r + t, :] = res[BATCH * t:BATCH * (t + 1), :]
            else:
                xo_ref[rs, :] = res

    win[1] = u_ref[0]


def _ffn_down_call(uv, xs, mods, conv_w, conv_b, w_down, ln_g, ln_b, l, who, on_grid, batch_major_out=False):
    n = xs.shape[0]
    tm = GRID_ROW
    n_tiles = n // tm
    n_chunks = D_FF // FF_CHUNK
    assert n_chunks == 2
    done = lambda s: jnp.maximum(s - 1, 0)
    row = pl.BlockSpec((tm, D_MODEL), lambda s, c: (done(s), 0))
    scratch = [pltpu.VMEM((n_chunks, 2, tm, FF_CHUNK), BF16), pltpu.VMEM((tm, FF_CHUNK), BF16)]
    out_spec, out_shape = row, jax.ShapeDtypeStruct((n, D_MODEL), F32)
    if batch_major_out:
        scratch.append(pltpu.VMEM((tm, D_MODEL), F32))
        out_spec = pl.BlockSpec((BATCH, tm // BATCH, D_MODEL), lambda s, c: (0, done(s), 0))
        out_shape = jax.ShapeDtypeStruct((BATCH, n // BATCH, D_MODEL), F32)
    return pl.pallas_call(
        functools.partial(_ffn_down_kernel, n_tiles=n_tiles, on_grid=on_grid),
        grid=(n_tiles + 1, n_chunks),
        in_specs=[pl.BlockSpec((1, tm, FF_CHUNK), lambda s, c: (0, jnp.minimum(s, n_tiles - 1), c)),
                  pl.BlockSpec((1, tm, FF_CHUNK), lambda s, c: (1, done(s), c)),
                  pl.BlockSpec((1, 9, FF_CHUNK), lambda s, c: (l, 0, c)),
                  pl.BlockSpec((1, 1, FF_CHUNK), lambda s, c: (l, 0, c)),
                  pl.BlockSpec((1, FF_CHUNK, D_MODEL), lambda s, c: (l, c, 0)),
                  row, _mod_spec(l, who, GATE2), _layer_spec((1, D_MODEL), l), _layer_spec((1, D_MODEL), l)],
        out_specs=out_spec,
        out_shape=out_shape,
        scratch_shapes=scratch,
        compiler_params=_cparams(("arbitrary", "arbitrary")),
        name="ffn_down_grid" if on_grid else "ffn_down_seq",
    )(uv, uv, conv_w, conv_b, w_down, xs, mods, ln_g, ln_b)


def kernel(x, c, ctx, c_ctx, w_mod, b_mod, w_in, s5_lam_re, s5_lam_im, s5_log_dt, s5_b_re, s5_b_im, s5_c_re, s5_c_im, s5_d, s5_w_glu, s5_b_glu, lru_conv_w, lru_conv_b, lru_w_a, lru_b_a, lru_w_x, lru_b_x, lru_lam, w_out, ln1_g, ln1_b, ffn_w_up, ffn_conv_w, ffn_conv_b, ffn_w_down, ln2_g, ln2_b):
    xs, cs = x, ctx

    c_all = jnp.concatenate([c, jnp.broadcast_to(c_ctx, (BATCH, D_MODEL))], axis=0)
    mods = _mod_call(c_all, w_mod, b_mod)

    row = lambda v: v.reshape(DEPTH, 1, v.shape[-1])
    w_in_b = w_in.astype(BF16)
    mats = jax.vmap(_s5_matrices)(s5_lam_re, s5_lam_im, s5_log_dt, s5_b_re, s5_b_im, s5_c_re, s5_c_im)
    eye_h = jnp.eye(LRU_HEADS, dtype=F32)
    dense = lambda wh: jnp.einsum("ldhij,hk->ldhikj", wh, eye_h).reshape(DEPTH, 2, D_LRU, D_LRU)
    wg = (0.5 * jnp.concatenate([dense(lru_w_a), dense(lru_w_x)], axis=-1)).astype(BF16)
    bg = 0.5 * jnp.concatenate([lru_b_a, lru_b_x], axis=-1).reshape(DEPTH, 2, 1, 2 * D_LRU)
    lam = lru_lam.reshape(DEPTH, 2, 1, D_LRU)
    w_glu = s5_w_glu.astype(BF16)
    w_out_b = w_out.astype(BF16)
    w_up = ffn_w_up.astype(BF16)
    f_conv_w = ffn_conv_w.reshape(DEPTH, 9, D_FF)
    w_down = ffn_w_down.astype(BF16)
    ln1 = (row(ln1_g), row(ln1_b))
    ln2 = (row(ln2_g), row(ln2_b))

    s5_zero = jnp.zeros((2, S5_LANE_BLOCKS, BATCH, S5_STATE_COLS), F32)
    lru_zero = jnp.zeros((2, BATCH, D_LRU), F32)

    for l in range(DEPTH):
        last = l == DEPTH - 1

        def mixers(stream, who, s5_init, lru_init):
            u, xl, gl, *tm_copy = _mix_in_call(stream, mods, w_in_b, l, who, batch_major=(l == 0))
            ys, s5_fin = _s5_call(u, mats, s5_init, l)
            hl, lru_fin = _lru_call(xl, lru_conv_w, row(lru_conv_b), wg, bg, lam, lru_init, l)
            return (ys, u, hl, gl), (s5_fin, lru_fin), (tm_copy[0] if tm_copy else stream)

        def finish(stream, mixed, who, on_grid):
            xs_mid = _mix_out_call(*mixed, stream, mods, row(s5_d), w_glu, row(s5_b_glu), w_out_b, *ln1, l, who)
            uv = _ffn_up_call(xs_mid, mods, w_up, l, who)
            return _ffn_down_call(uv, xs_mid, mods, f_conv_w, row(ffn_conv_b), w_down, *ln2, l, who, on_grid,
                                  batch_major_out=last)

        mixed_c, (s5_fin, lru_fin), cs = mixers(cs, 1, s5_zero, lru_zero)
        mixed_x, _, xs = mixers(xs, 0, s5_fin, lru_fin)
        xs = finish(xs, mixed_x, 0, True)
        if not last:
            cs = finish(cs, mixed_c, 1, False)

    return xs
```

```python
import functools

import jax
import jax.numpy as jnp
from jax import lax
from jax.experimental import pallas as pl
from jax.experimental.pallas import tpu as pltpu

F32 = jnp.float32
BF16 = jnp.bfloat16

D_MODEL = 1024
BATCH = 16
DEPTH = 4
GRID_W = 64
D_S5 = 512
D_LRU = 512
S5_GROUP_CH = 16
S5_STATE = 64
LRU_HEADS = 8
LRU_C = 8.0
D_FF = 2816
N_MOD = 6
LN_EPS = 1e-6
DEEPNORM_ALPHA = (2 * DEPTH) ** 0.25

LANES = 128
GROUPS_PER_LANE_BLOCK = LANES // S5_GROUP_CH
S5_LANE_BLOCKS = D_S5 // LANES
S5_STATE_COLS = 2 * GROUPS_PER_LANE_BLOCK * S5_STATE
S5_CHUNK = 8
S5_TK = S5_CHUNK * LANES
MOD_ROWS = 2 * BATCH
ROW_BLOCK = 256
FF_CHUNK = 1408
VMEM_LIMIT = 60 * 1024 * 1024
SHIFT1, SCALE1, GATE1, SHIFT2, SCALE2, GATE2 = range(N_MOD)


def _cparams(sem):
    return pltpu.CompilerParams(dimension_semantics=sem, vmem_limit_bytes=VMEM_LIMIT)


def _layer_spec(shape, l):
    return pl.BlockSpec((1,) + shape, lambda *_: (l,) + (0,) * len(shape))


def _mod_spec(l, who, slot):
    return pl.BlockSpec((1, BATCH, D_MODEL), lambda *_: (l, who, slot))


def _sigmoid(x):
    return 0.5 * jnp.tanh(0.5 * x) + 0.5


def _gelu(x):
    return 0.5 * x * (1.0 + jnp.tanh(0.7978845608028654 * (x + 0.044715 * (x * x * x))))


def _ln(x):
    mu = jnp.mean(x, axis=-1, keepdims=True)
    xc = x - mu
    var = jnp.mean(xc * xc, axis=-1, keepdims=True)
    return xc * lax.rsqrt(var + LN_EPS)


def _per_batch(x, fn):
    rows, d = x.shape
    return fn(x.reshape(rows // BATCH, BATCH, d)).reshape(rows, d)


def _modulate(x, shift, scale):
    return _per_batch(_ln(x), lambda h: h * (1.0 + scale)[None] + shift[None])


def _mod_kernel(c_ref, w_ref, b_ref, o_ref):
    c = c_ref[...]
    s = c * _sigmoid(c)
    o_ref[0] = jnp.dot(s, w_ref[0], preferred_element_type=F32) + b_ref[0]


def _mod_call(c_all, w_mod, b_mod):
    tn = 1536
    return pl.pallas_call(
        _mod_kernel,
        grid=(DEPTH, N_MOD * D_MODEL // tn),
        in_specs=[pl.BlockSpec((MOD_ROWS, D_MODEL), lambda l, j: (0, 0)),
                  pl.BlockSpec((1, D_MODEL, tn), lambda l, j: (l, 0, j)),
                  pl.BlockSpec((1, 1, tn), lambda l, j: (l, 0, j))],
        out_specs=pl.BlockSpec((1, MOD_ROWS, tn), lambda l, j: (l, 0, j)),
        out_shape=jax.ShapeDtypeStruct((DEPTH, MOD_ROWS, N_MOD * D_MODEL), F32),
        compiler_params=_cparams(("parallel", "parallel")),
        name="mod",
    )(c_all, w_mod, b_mod.reshape(DEPTH, 1, N_MOD * D_MODEL))


def _mix_in_kernel(x_ref, sh_ref, sc_ref, w_ref, u_ref, xl_ref, gl_ref, *xs_out):
    rows = u_ref.shape[0]
    steps = ROW_BLOCK // BATCH
    for r in range(rows // ROW_BLOCK):
        rs = slice(ROW_BLOCK * r, ROW_BLOCK * (r + 1))
        if xs_out:
            x = jnp.concatenate([x_ref[:, steps * r + t, :] for t in range(steps)], axis=0)
            xs_out[0][rs, :] = x
        else:
            x = x_ref[rs, :]
        h = _modulate(x, sh_ref[0], sc_ref[0]).astype(BF16)
        p = jnp.dot(h, w_ref[0], preferred_element_type=F32)
        u_ref[rs, :] = p[:, :D_S5].astype(BF16)
        xl_ref[rs, :] = p[:, D_S5:D_S5 + D_LRU].astype(BF16)
        gl_ref[rs, :] = p[:, D_S5 + D_LRU:].astype(BF16)


def _mix_in_call(xs, mods, w_in, l, who, batch_major=False):
    n = xs.shape[0] * xs.shape[1] if batch_major else xs.shape[0]
    tm = 1024
    out = pl.BlockSpec((tm, D_S5), lambda i: (i, 0))
    out_specs = [out, out, out]
    out_shape = [jax.ShapeDtypeStruct((n, D_S5), BF16)] * 3
    x_spec = pl.BlockSpec((tm, D_MODEL), lambda i: (i, 0))
    if batch_major:
        x_spec = pl.BlockSpec((BATCH, tm // BATCH, D_MODEL), lambda i: (0, i, 0))
        out_specs.append(pl.BlockSpec((tm, D_MODEL), lambda i: (i, 0)))
        out_shape.append(jax.ShapeDtypeStruct((n, D_MODEL), F32))
    return pl.pallas_call(
        _mix_in_kernel,
        grid=(n // tm,),
        in_specs=[x_spec, _mod_spec(l, who, SHIFT1), _mod_spec(l, who, SCALE1),
                  _layer_spec((D_MODEL, D_S5 + 2 * D_LRU), l)],
        out_specs=out_specs,
        out_shape=out_shape,
        compiler_params=_cparams(("parallel",)),
        name="mix_in",
    )(xs, mods, mods, w_in)


def _s5_matrices(lam_re, lam_im, log_dt, b_re, b_im, c_re, c_im):
    T = S5_CHUNK
    lam = lax.complex(lam_re, lam_im)
    lam_dt = lam * jnp.exp(log_dt)[..., None]
    lam_bar = jnp.exp(lam_dt)
    b_bar = ((lam_bar - 1.0) / lam)[..., None] * lax.complex(b_re, b_im)
    c_mat = lax.complex(c_re, c_im)
    steps = jnp.arange(T + 1, dtype=F32)
    pw = jnp.exp(steps[:, None, None, None] * lam_dt[None])
    kd = jnp.real(jnp.einsum("dgop,ndgp,dgpi->ndgio", c_mat, pw[:T], b_bar))
    nb, gpb = S5_LANE_BLOCKS, GROUPS_PER_LANE_BLOCK
    half = gpb * S5_STATE

    lane_group = jnp.arange(LANES) // S5_GROUP_CH
    state_group = jnp.arange(half) // S5_STATE
    rep_ch = jnp.tile(jnp.eye(S5_GROUP_CH, dtype=BF16), (1, gpb))
    rep_st = jnp.tile(jnp.eye(S5_STATE, dtype=BF16), (1, gpb))
    same_ll = (lane_group[:, None] == lane_group[None, :]).astype(BF16)
    same_ls = (lane_group[:, None] == state_group[None, :]).astype(BF16)
    same_sl = same_ls.T

    def spread(x, rep, same, out):
        return jnp.einsum(out, x.astype(BF16), rep, preferred_element_type=BF16) * same

    ar = jnp.arange(T)
    kd = kd.at[0, 0].add(kd[0, 1])
    kb = spread(kd.reshape(T, 2, nb, LANES, S5_GROUP_CH), rep_ch, same_ll, "ndjxo,oq->dnjxq")
    m = jnp.stack([jnp.concatenate([kb[0, b - a] if b >= a else kb[1, a - b] for b in range(T)], axis=-1)
                   for a in range(T)], axis=1).reshape(nb, S5_TK, S5_TK)

    e_in = jnp.stack([T - 1 - ar, ar])
    pw_in = jnp.stack([pw[e_in[d], d] for d in range(2)])
    wc = (pw_in[:, :, :, None, :] * jnp.swapaxes(b_bar, -1, -2)[:, None]).reshape(2, T, nb, LANES, S5_STATE)
    w = jnp.concatenate([spread(part, rep_st, same_ls, "dtjxp,pq->djtxq")
                         for part in (jnp.real(wc), jnp.imag(wc))], axis=-1)
    w = w.reshape(2, nb, S5_TK, S5_STATE_COLS)

    f_out = jnp.stack([ar + 1, T - ar])
    pw_out = jnp.stack([pw[f_out[d], d] for d in range(2)])
    zc = (jnp.swapaxes(c_mat, -1, -2)[:, None] * pw_out[..., None]).reshape(2, T, nb, half, S5_GROUP_CH)
    z = jnp.concatenate([
        jnp.concatenate([spread(part[:, t], rep_ch, same_sl, "djyo,oq->djyq") for t in range(T)], axis=-1)
        for part in (jnp.real(zc), -jnp.imag(zc))], axis=-2)

    a_t = pw[T].reshape(2, nb, half)
    a = jnp.stack([jnp.real(a_t), jnp.imag(a_t)], axis=2)
    return w, m, z, a


def _s5_kernel(u_ref, w_ref, m_ref, z_ref, a_ref, init_ref, y_ref, fin_ref, s_ref, v_ref, spf_ref, spb_ref, *,
               kc, n_tiles):
    ph = pl.program_id(1)
    rt = pl.program_id(2)
    half = S5_STATE_COLS // 2
    rows = kc * BATCH

    @pl.when((ph == 0) & (rt == 0))
    def _():
        s_ref[...] = init_ref[:, 0]

    ucat = jnp.concatenate(
        [u_ref[:, BATCH * t:BATCH * (t + 1), :].reshape(rows, LANES) for t in range(S5_CHUNK)], axis=1)

    def recurrence(d, sp_ref, base):
        v_ref[...] = jnp.dot(ucat, w_ref[0, 0, 0], preferred_element_type=F32)
        a_re = jnp.broadcast_to(a_ref[0, d, 0, 0:1, :], (BATCH, half))
        a_im = jnp.broadcast_to(a_ref[0, d, 0, 1:2, :], (BATCH, half))

        def step(i, carry):
            s_re, s_im = carry
            off = pl.multiple_of((kc - 1 - i if d else i) * BATCH, BATCH)
            dst = pl.multiple_of(base + off, BATCH)
            sp_ref[pl.ds(dst, BATCH), :half] = s_re.astype(BF16)
            sp_ref[pl.ds(dst, BATCH), half:] = s_im.astype(BF16)
            v = v_ref[pl.ds(off, BATCH), :]
            n_re = a_re * s_re - a_im * s_im + v[:, :half]
            n_im = a_re * s_im + a_im * s_re + v[:, half:]
            return n_re, n_im

        s_re, s_im = lax.fori_loop(0, kc, step, (s_ref[d, :, :half], s_ref[d, :, half:]))
        s_ref[d, :, :half] = s_re
        s_ref[d, :, half:] = s_im

        @pl.when(rt == n_tiles - 1)
        def _():
            fin_ref[d, 0] = s_ref[d]

    @pl.when(ph == 0)
    def _():
        recurrence(1, spb_ref, pl.multiple_of((n_tiles - 1 - rt) * rows, rows))

    @pl.when(ph == 1)
    def _():
        recurrence(0, spf_ref, 0)
        s_bwd = spb_ref[pl.ds(pl.multiple_of(rt * rows, rows), rows), :]
        y = (jnp.dot(ucat, m_ref[0, 0], preferred_element_type=F32)
             + jnp.dot(spf_ref[...], z_ref[0, 0, 0], preferred_element_type=F32)
             + jnp.dot(s_bwd, z_ref[0, 1, 0], preferred_element_type=F32))
        for t in range(S5_CHUNK):
            y_ref[:, BATCH * t:BATCH * (t + 1), :] = (
                y[:, LANES * t:LANES * (t + 1)].reshape(kc, BATCH, LANES).astype(BF16))


def _s5_call(u, mats, init, l):
    w, m, z, a = mats
    n = u.shape[0]
    chunk_rows = S5_CHUNK * BATCH
    n_chunks = n // chunk_rows
    kc = min(64, n_chunks)
    n_tiles = n_chunks // kc
    nb = S5_LANE_BLOCKS

    def tile(ph, rt):
        return rt + (1 - ph) * (n_tiles - 1 - 2 * rt)

    both_dirs = lambda shape: pl.BlockSpec((2, 1) + shape, lambda j, ph, rt: (0, j, 0, 0))
    y, fin = pl.pallas_call(
        functools.partial(_s5_kernel, kc=kc, n_tiles=n_tiles),
        grid=(nb, 2, n_tiles),
        in_specs=[pl.BlockSpec((kc, chunk_rows, LANES), lambda j, ph, rt: (tile(ph, rt), 0, j)),
                  pl.BlockSpec((1, 1, 1, S5_TK, S5_STATE_COLS), lambda j, ph, rt: (l, 1 - ph, j, 0, 0)),
                  pl.BlockSpec((1, 1, S5_TK, S5_TK), lambda j, ph, rt: (l, j, 0, 0)),
                  pl.BlockSpec((1, 2, 1, S5_STATE_COLS, S5_TK), lambda j, ph, rt: (l, 0, j, 0, 0)),
                  pl.BlockSpec((1, 2, 1, 2, S5_STATE_COLS // 2), lambda j, ph, rt: (l, 0, j, 0, 0)),
                  both_dirs((BATCH, S5_STATE_COLS))],
        out_specs=[pl.BlockSpec((kc, chunk_rows, LANES), lambda j, ph, rt: (ph * rt, 0, j)),
                   both_dirs((BATCH, S5_STATE_COLS))],
        out_shape=[jax.ShapeDtypeStruct((n_chunks, chunk_rows, D_S5), BF16),
                   jax.ShapeDtypeStruct((2, nb, BATCH, S5_STATE_COLS), F32)],
        scratch_shapes=[pltpu.VMEM((2, BATCH, S5_STATE_COLS), F32),
                        pltpu.VMEM((kc * BATCH, S5_STATE_COLS), F32),
                        pltpu.VMEM((kc * BATCH, S5_STATE_COLS), BF16),
                        pltpu.VMEM((n_chunks * BATCH, S5_STATE_COLS), BF16)],
        compiler_params=_cparams(("arbitrary", "arbitrary", "arbitrary")),
        name="s5",
    )(u.reshape(n_chunks, chunk_rows, D_S5), w, m, z, a, init)
    return y.reshape(n, D_S5), fin


LRU_HALO_PREV = 2 * BATCH
LRU_HALO_NEXT = BATCH


def _lru_kernel(xp_ref, xc_ref, xn_ref, cw_ref, cb_ref, wg_ref, bg_ref, lam_ref, init_ref,
                h_ref, fin_ref, xpad_ref, a_ref, b_ref, hst_ref, *, tm, n_tiles):
    d = pl.program_id(0)
    i = pl.program_id(1)
    te = i + d * (n_tiles - 1 - 2 * i)

    @pl.when(i == 0)
    def _():
        hst_ref[...] = init_ref[0]

    zero = jnp.zeros((), BF16)
    xpad_ref[0:LRU_HALO_PREV, :] = jnp.where(te > 0, xp_ref[...], zero)
    xpad_ref[LRU_HALO_PREV:LRU_HALO_PREV + tm, :] = xc_ref[...]
    xpad_ref[LRU_HALO_PREV + tm:, :] = jnp.where(te < n_tiles - 1, xn_ref[...], zero)

    xc_b = cb_ref[0].astype(BF16)
    for k in range(4):
        xc_b = xc_b + cw_ref[0, k:k + 1, :].astype(BF16) * xpad_ref[BATCH * k:BATCH * k + tm, :]
    xc = xc_b.astype(F32)

    gates = jnp.tanh(jnp.dot(xc_b, wg_ref[0, 0], preferred_element_type=F32) + bg_ref[0, 0])
    z = -lam_ref[0, 0]
    rate = (0.5 * LRU_C) * (jnp.maximum(z, 0.0) + jnp.log1p(jnp.exp(-jnp.abs(z))))
    neg_log_a = rate * gates[:, :D_LRU] + rate
    a = jnp.exp2(neg_log_a * (-1.4426950408889634))
    half_x = 0.5 * xc
    gated_x = half_x * gates[:, D_LRU:] + half_x
    a_ref[...] = a
    b_ref[...] = jnp.sqrt(jnp.tanh(neg_log_a) * (a * a + 1.0)) * gated_x

    steps = tm // BATCH

    def step(j, h):
        k = j + d * (steps - 1 - 2 * j)
        off = pl.multiple_of(k * BATCH, BATCH)
        h = a_ref[pl.ds(off, BATCH), :] * h + b_ref[pl.ds(off, BATCH), :]
        h_ref[0, pl.ds(off, BATCH), :] = h.astype(BF16)
        return h

    h = lax.fori_loop(0, steps, step, hst_ref[...], unroll=4)
    hst_ref[...] = h

    @pl.when(i == n_tiles - 1)
    def _():
        fin_ref[0] = h


def _lru_call(xl, conv_w, conv_b, wg, bg, lam, init, l):
    n = xl.shape[0]
    tm = 2048
    n_tiles = n // tm

    def tile(d, i):
        return i + d * (n_tiles - 1 - 2 * i)

    prev_blocks = tm // LRU_HALO_PREV
    next_blocks = tm // LRU_HALO_NEXT
    layer_dir = lambda shape: pl.BlockSpec((1, 1) + shape, lambda d, i: (l, d) + (0,) * len(shape))
    per_dir = lambda shape: pl.BlockSpec((1,) + shape, lambda d, i: (d,) + (0,) * len(shape))
    return pl.pallas_call(
        functools.partial(_lru_kernel, tm=tm, n_tiles=n_tiles),
        grid=(2, n_tiles),
        in_specs=[pl.BlockSpec((LRU_HALO_PREV, D_LRU),
                               lambda d, i: (jnp.maximum(tile(d, i) * prev_blocks - 1, 0), 0)),
                  pl.BlockSpec((tm, D_LRU), lambda d, i: (tile(d, i), 0)),
                  pl.BlockSpec((LRU_HALO_NEXT, D_LRU),
                               lambda d, i: (jnp.minimum((tile(d, i) + 1) * next_blocks, n // LRU_HALO_NEXT - 1), 0)),
                  _layer_spec((4, D_LRU), l), _layer_spec((1, D_LRU), l),
                  layer_dir((D_LRU, 2 * D_LRU)), layer_dir((1, 2 * D_LRU)), layer_dir((1, D_LRU)),
                  per_dir((BATCH, D_LRU))],
        out_specs=[pl.BlockSpec((1, tm, D_LRU), lambda d, i: (d, tile(d, i), 0)),
                   per_dir((BATCH, D_LRU))],
        out_shape=[jax.ShapeDtypeStruct((2, n, D_LRU), BF16),
                   jax.ShapeDtypeStruct((2, BATCH, D_LRU), F32)],
        scratch_shapes=[pltpu.VMEM((LRU_HALO_PREV + tm + LRU_HALO_NEXT, D_LRU), BF16),
                        pltpu.VMEM((tm, D_LRU), F32),
                        pltpu.VMEM((tm, D_LRU), F32),
                        pltpu.VMEM((BATCH, D_LRU), F32)],
        compiler_params=_cparams(("arbitrary", "arbitrary")),
        name="lru",
    )(xl, xl, xl, conv_w, conv_b, wg, bg, lam, init)


def _mix_out_kernel(ys_ref, u_ref, hl_ref, gl_ref, xs_ref, d_ref, wglu_ref, bglu_ref, wout_ref,
                    g1_ref, lng_ref, lnb_ref, xo_ref):
    g1 = g1_ref[0]
    for r in range(xs_ref.shape[0] // ROW_BLOCK):
        rs = slice(ROW_BLOCK * r, ROW_BLOCK * (r + 1))
        y = (ys_ref[rs, :].astype(F32) + d_ref[0] * u_ref[rs, :].astype(F32)).astype(BF16)
        y = _gelu(y)
        glu = jnp.dot(y, wglu_ref[0], preferred_element_type=F32) + bglu_ref[0]
        y = y * _sigmoid(glu.astype(BF16))
        y_lru = (hl_ref[0, rs, :] + hl_ref[1, rs, :]) * _gelu(gl_ref[rs, :])
        out = jnp.dot(y, wout_ref[0, :D_S5, :], preferred_element_type=F32)
        out = out + jnp.dot(y_lru, wout_ref[0, D_S5:, :], preferred_element_type=F32)
        zsum = DEEPNORM_ALPHA * xs_ref[rs, :] + _per_batch(out, lambda o: o * g1[None])
        xo_ref[rs, :] = _ln(zsum) * lng_ref[0] + lnb_ref[0]


def _mix_out_call(ys, u, hl, gl, xs, mods, s5_d, w_glu, b_glu, w_out, ln_g, ln_b, l, who):
    n = xs.shape[0]
    tm = 1024
    half = pl.BlockSpec((tm, D_S5), lambda i: (i, 0))
    both = pl.BlockSpec((2, tm, D_S5), lambda i: (0, i, 0))
    row = pl.BlockSpec((tm, D_MODEL), lambda i: (i, 0))
    return pl.pallas_call(
        _mix_out_kernel,
        grid=(n // tm,),
        in_specs=[half, half, both, half, row,
                  _layer_spec((1, D_S5), l), _layer_spec((D_S5, D_S5), l), _layer_spec((1, D_S5), l),
                  _layer_spec((D_MODEL, D_MODEL), l), _mod_spec(l, who, GATE1),
                  _layer_spec((1, D_MODEL), l), _layer_spec((1, D_MODEL), l)],
        out_specs=row,
        out_shape=jax.ShapeDtypeStruct((n, D_MODEL), F32),
        compiler_params=_cparams(("parallel",)),
        name="mix_out",
    )(ys, u, hl, gl, xs, s5_d, w_glu, b_glu, w_out, mods, ln_g, ln_b)


UP_ROWS = 256


def _ffn_up_kernel(x_ref, sh_ref, sc_ref, w_ref, o_ref):
    for r in range(x_ref.shape[0] // UP_ROWS):
        rs = slice(UP_ROWS * r, UP_ROWS * (r + 1))
        h = _modulate(x_ref[rs, :], sh_ref[0], sc_ref[0]).astype(BF16)
        for half in range(2):
            o_ref[half, rs, :] = jnp.dot(h, w_ref[0, :, D_FF * half:D_FF * (half + 1)],
                                         preferred_element_type=F32).astype(BF16)


def _ffn_up_call(xs, mods, w_up, l, who):
    n = xs.shape[0]
    tm = 1024
    return pl.pallas_call(
        _ffn_up_kernel,
        grid=(n // tm,),
        in_specs=[pl.BlockSpec((tm, D_MODEL), lambda i: (i, 0)),
                  _mod_spec(l, who, SHIFT2), _mod_spec(l, who, SCALE2),
                  pl.BlockSpec((1, D_MODEL, 2 * D_FF), lambda i: (l, 0, 0), pipeline_mode=pl.Buffered(1))],
        out_specs=pl.BlockSpec((2, tm, D_FF), lambda i: (0, i, 0)),
        out_shape=jax.ShapeDtypeStruct((2, n, D_FF), BF16),
        compiler_params=_cparams(("parallel",)),
        name="ffn_up",
    )(xs, mods, mods, w_up)


GRID_ROW = GRID_W * BATCH
EPILOGUE_ROWS = 256
CONV_ROWS = 512
_GELU_K = 0.7978845608028654


def _half_gelu_gate(h, v):
    hv = h * v
    return hv + hv * jnp.tanh(h * (2.0 * _GELU_K + (8.0 * 0.044715 * _GELU_K) * (h * h)))


def _ffn_down_kernel(u_ref, v_ref, cw_ref, cb_ref, wd_ref, xs_ref, g2_ref, lng_ref, lnb_ref,
                     xo_ref, win_ref, act_ref, *acc, n_tiles, on_grid):
    s = pl.program_id(0)
    c = pl.program_id(1)
    tm = GRID_ROW
    win = win_ref.at[c]
    acc_ref = acc[0] if acc else xo_ref

    @pl.when(s == 0)
    def _():
        win[0] = jnp.zeros((tm, FF_CHUNK), BF16)

    def finish_tile(accumulate):
        edge = jnp.zeros((BATCH, LANES), BF16)
        for r in range(tm // CONV_ROWS):
            r0, r1 = CONV_ROWS * r, CONV_ROWS * (r + 1)
            lo, hi = max(r0 - BATCH, 0), min(r1 + BATCH, tm)
            for k in range(FF_CHUNK // LANES):
                ls = slice(LANES * k, LANES * (k + 1))
                w = [(0.5 * cw_ref[0, t:t + 1, ls]).astype(BF16) for t in range(9)]
                nxt = lambda a, b: jnp.where(s < n_tiles, u_ref[0, a:b, ls], jnp.zeros((), BF16))
                if on_grid:
                    slabs = ((-1, win[0, lo:hi, ls], edge, edge), (0, win[1, lo:hi, ls], edge, edge),
                             (1, nxt(lo, hi), edge, edge))
                else:
                    slabs = ((0, win[1, lo:hi, ls], win[0, tm - BATCH:, ls], nxt(0, BATCH)),)
                conv = (0.5 * cb_ref[0, :, ls]).astype(BF16)
                for dy, ext, before, after in slabs:
                    if r0 == 0:
                        ext = jnp.concatenate([before, ext], axis=0)
                    if r1 == tm:
                        ext = jnp.concatenate([ext, after], axis=0)
                    t0 = 3 * (dy + 1)
                    conv = conv + (w[t0] * ext[:CONV_ROWS] + w[t0 + 1] * ext[BATCH:BATCH + CONV_ROWS]
                                   + w[t0 + 2] * ext[2 * BATCH:])
                act_ref[r0:r1, ls] = _half_gelu_gate(conv, v_ref[0, r0:r1, ls])
            part = jnp.dot(act_ref[r0:r1, :], wd_ref[0], preferred_element_type=F32)
            if accumulate:
                acc_ref[r0:r1, :] += part
            else:
                acc_ref[r0:r1, :] = part
        win[0] = win[1]

    @pl.when((s > 0) & (c == 0))
    def _():
        finish_tile(False)

    @pl.when((s > 0) & (c == 1))
    def _():
        finish_tile(True)
        g2 = g2_ref[0]
        steps = EPILOGUE_ROWS // BATCH
        for r in range(tm // EPILOGUE_ROWS):
            rs = slice(EPILOGUE_ROWS * r, EPILOGUE_ROWS * (r + 1))
            zsum = DEEPNORM_ALPHA * xs_ref[rs, :] + _per_batch(acc_ref[rs, :], lambda o: o * g2[None])
            res = _ln(zsum) * lng_ref[0] + lnb_ref[0]
            if acc:
                for t in range(steps):
                    xo_ref[:, steps * r + t, :] = res[BATCH * t:BATCH * (t + 1), :]
            else:
                xo_ref[rs, :] = res

    win[1] = u_ref[0]


def _ffn_down_call(uv, xs, mods, conv_w, conv_b, w_down, ln_g, ln_b, l, who, on_grid, batch_major_out=False):
    n = xs.shape[0]
    tm = GRID_ROW
    n_tiles = n // tm
    n_chunks = D_FF // FF_CHUNK
    assert n_chunks == 2
    done = lambda s: jnp.maximum(s - 1, 0)
    row = pl.BlockSpec((tm, D_MODEL), lambda s, c: (done(s), 0))
    scratch = [pltpu.VMEM((n_chunks, 2, tm, FF_CHUNK), BF16), pltpu.VMEM((tm, FF_CHUNK), BF16)]
    out_spec, out_shape = row, jax.ShapeDtypeStruct((n, D_MODEL), F32)
    if batch_major_out:
        scratch.append(pltpu.VMEM((tm, D_MODEL), F32))
        out_spec = pl.BlockSpec((BATCH, tm // BATCH, D_MODEL), lambda s, c: (0, done(s), 0))
        out_shape = jax.ShapeDtypeStruct((BATCH, n // BATCH, D_MODEL), F32)
    return pl.pallas_call(
        functools.partial(_ffn_down_kernel, n_tiles=n_tiles, on_grid=on_grid),
        grid=(n_tiles + 1, n_chunks),
        in_specs=[pl.BlockSpec((1, tm, FF_CHUNK), lambda s, c: (0, jnp.minimum(s, n_tiles - 1), c)),
                  pl.BlockSpec((1, tm, FF_CHUNK), lambda s, c: (1, done(s), c)),
                  pl.BlockSpec((1, 9, FF_CHUNK), lambda s, c: (l, 0, c)),
                  pl.BlockSpec((1, 1, FF_CHUNK), lambda s, c: (l, 0, c)),
                  pl.BlockSpec((1, FF_CHUNK, D_MODEL), lambda s, c: (l, c, 0)),
                  row, _mod_spec(l, who, GATE2), _layer_spec((1, D_MODEL), l), _layer_spec((1, D_MODEL), l)],
        out_specs=out_spec,
        out_shape=out_shape,
        scratch_shapes=scratch,
        compiler_params=_cparams(("arbitrary", "arbitrary")),
        name="ffn_down_grid" if on_grid else "ffn_down_seq",
    )(uv, uv, conv_w, conv_b, w_down, xs, mods, ln_g, ln_b)


def kernel(x, c, ctx, c_ctx, w_mod, b_mod, w_in, s5_lam_re, s5_lam_im, s5_log_dt, s5_b_re, s5_b_im, s5_c_re, s5_c_im, s5_d, s5_w_glu, s5_b_glu, lru_conv_w, lru_conv_b, lru_w_a, lru_b_a, lru_w_x, lru_b_x, lru_lam, w_out, ln1_g, ln1_b, ffn_w_up, ffn_conv_w, ffn_conv_b, ffn_w_down, ln2_g, ln2_b):
    xs, cs = x, ctx

    c_all = jnp.concatenate([c, jnp.broadcast_to(c_ctx, (BATCH, D_MODEL))], axis=0)
    mods = _mod_call(c_all, w_mod, b_mod)

    row = lambda v: v.reshape(DEPTH, 1, v.shape[-1])
    w_in_b = w_in.astype(BF16)
    mats = jax.vmap(_s5_matrices)(s5_lam_re, s5_lam_im, s5_log_dt, s5_b_re, s5_b_im, s5_c_re, s5_c_im)
    eye_h = jnp.eye(LRU_HEADS, dtype=F32)
    dense = lambda wh: jnp.einsum("ldhij,hk->ldhikj", wh, eye_h).reshape(DEPTH, 2, D_LRU, D_LRU)
    wg = (0.5 * jnp.concatenate([dense(lru_w_a), dense(lru_w_x)], axis=-1)).astype(BF16)
    bg = 0.5 * jnp.concatenate([lru_b_a, lru_b_x], axis=-1).reshape(DEPTH, 2, 1, 2 * D_LRU)
    lam = lru_lam.reshape(DEPTH, 2, 1, D_LRU)
    w_glu = s5_w_glu.astype(BF16)
    w_out_b = w_out.astype(BF16)
    w_up = ffn_w_up.astype(BF16)
    f_conv_w = ffn_conv_w.reshape(DEPTH, 9, D_FF)
    w_down = ffn_w_down.astype(BF16)
    ln1 = (row(ln1_g), row(ln1_b))
    ln2 = (row(ln2_g), row(ln2_b))

    s5_zero = jnp.zeros((2, S5_LANE_BLOCKS, BATCH, S5_STATE_COLS), F32)
    lru_zero = jnp.zeros((2, BATCH, D_LRU), F32)

    for l in range(DEPTH):
        last = l == DEPTH - 1

        def mixers(stream, who, s5_init, lru_init):
            u, xl, gl, *tm_copy = _mix_in_call(stream, mods, w_in_b, l, who, batch_major=(l == 0))
            ys, s5_fin = _s5_call(u, mats, s5_init, l)
            hl, lru_fin = _lru_call(xl, lru_conv_w, row(lru_conv_b), wg, bg, lam, lru_init, l)
            return (ys, u, hl, gl), (s5_fin, lru_fin), (tm_copy[0] if tm_copy else stream)

        def finish(stream, mixed, who, on_grid):
            xs_mid = _mix_out_call(*mixed, stream, mods, row(s5_d), w_glu, row(s5_b_glu), w_out_b, *ln1, l, who)
            uv = _ffn_up_call(xs_mid, mods, w_up, l, who)
            return _ffn_down_call(uv, xs_mid, mods, f_conv_w, row(ffn_conv_b), w_down, *ln2, l, who, on_grid,
                                  batch_major_out=last)

        mixed_c, (s5_fin, lru_fin), cs = mixers(cs, 1, s5_zero, lru_zero)
        mixed_x, _, xs = mixers(xs, 0, s5_fin, lru_fin)
        xs = finish(xs, mixed_x, 0, True)
        if not last:
            cs = finish(cs, mixed_c, 1, False)

    return xs
```
